```python
import math
import jax, jax.numpy as jnp
from jax import lax
import numpy as np

D_MODEL = 1024
BATCH = 8
SEQ = 2048
DEPTH = 4

D_MIX = D_MODEL
ATTN_WIDTH = D_MIX // 2
HEAD_DIM = 64
N_HEADS = ATTN_WIDTH // HEAD_DIM
CONV_WIDTH = D_MIX - ATTN_WIDTH
CONV_GROUPS = CONV_WIDTH // HEAD_DIM
CONV_K = 31
D_IN = 3 * ATTN_WIDTH + N_HEADS + 2 * CONV_WIDTH
D_FF = int(math.ceil((8 * D_MODEL / 3) / 256) * 256)
Q_BLOCK = 128
N_MOD = 6
EPS = 1e-6

kernel_name = "fox_conformer_hymba_adaln_trunk"


def rms_norm(x, g):
    xf = x.astype(jnp.float32)
    y = xf * lax.rsqrt(jnp.mean(xf * xf, axis=-1, keepdims=True) + EPS)
    return (y * g.astype(jnp.float32)).astype(x.dtype)


def layer_norm(x, g, b):
    xf = x.astype(jnp.float32)
    mu = jnp.mean(xf, axis=-1, keepdims=True)
    var = jnp.mean(jnp.square(xf - mu), axis=-1, keepdims=True)
    y = (xf - mu) * lax.rsqrt(var + EPS)
    return (y * g.astype(jnp.float32) + b.astype(jnp.float32)).astype(x.dtype)


def fox_attention(q, k, v, log_f):
    B, S, H, Dh = q.shape
    nb = S // Q_BLOCK
    scale = Dh ** -0.5
    cum = jnp.cumsum(log_f, axis=1)
    cum_k = jnp.transpose(cum, (0, 2, 1))[:, :, None, :]
    q_blocks = jnp.transpose(q.reshape(B, nb, Q_BLOCK, H, Dh), (1, 0, 2, 3, 4))
    c_blocks = jnp.transpose(cum.reshape(B, nb, Q_BLOCK, H), (1, 0, 3, 2))
    k_pos = jnp.arange(S, dtype=jnp.int32)

    def one_block(args):
        i, q_blk, c_blk = args
        s = jnp.einsum('bqhd,bkhd->bhqk', q_blk, k).astype(jnp.float32) * scale
        s = s + c_blk[..., None] - cum_k
        q_pos = i * Q_BLOCK + jnp.arange(Q_BLOCK, dtype=jnp.int32)
        mask = k_pos[None, :] <= q_pos[:, None]
        s = jnp.where(mask[None, None], s, -jnp.inf)
        p = jax.nn.softmax(s, axis=-1)
        return jnp.einsum('bhqk,bkhd->bqhd', p.astype(v.dtype), v)

    outs = lax.map(one_block, (jnp.arange(nb, dtype=jnp.int32), q_blocks, c_blocks))
    return jnp.transpose(outs, (1, 0, 2, 3, 4)).reshape(B, S, H * Dh)


def causal_depthwise_conv(u, w, b):
    C = u.shape[-1]
    out = lax.conv_general_dilated(
        u, w[:, None, :].astype(u.dtype), window_strides=(1,), padding=((CONV_K - 1, 0),),
        dimension_numbers=('NWC', 'WIO', 'NWC'), feature_group_count=C)
    return out + b.astype(u.dtype)


def hybrid_mixer(h, w_in, b_f, conv_w, conv_b, conv_ln_g, conv_ln_b, w_o):
    B, S, _ = h.shape
    proj = h @ w_in
    splits = [ATTN_WIDTH, 2 * ATTN_WIDTH, 3 * ATTN_WIDTH, 3 * ATTN_WIDTH + N_HEADS,
              3 * ATTN_WIDTH + N_HEADS + CONV_WIDTH]
    q, k, v, f_logit, conv_val, conv_gate = jnp.split(proj, splits, axis=-1)
    q = q.reshape(B, S, N_HEADS, HEAD_DIM)
    k = k.reshape(B, S, N_HEADS, HEAD_DIM)
    v = v.reshape(B, S, N_HEADS, HEAD_DIM)
    log_f = jax.nn.log_sigmoid((f_logit + b_f).astype(jnp.float32))
    attn = fox_attention(q, k, v, log_f)
    u = conv_val * jax.nn.sigmoid(conv_gate)
    u = causal_depthwise_conv(u, conv_w, conv_b)
    u = jax.nn.silu(layer_norm(u, conv_ln_g, conv_ln_b))
    return jnp.concatenate([attn, u], axis=-1) @ w_o


def swiglu_ffn(h, w_ffn_in, w_ffn_out):
    g, u = jnp.split(h @ w_ffn_in, 2, axis=-1)
    return (jax.nn.silu(g) * u) @ w_ffn_out


def setup_inputs(seed: int = 0) -> dict:
    key = jax.random.key(seed)
    ks = jax.random.split(key, 24)
    f32 = jnp.float32
    L, D = DEPTH, D_MODEL

    def nrm(k, shape, s):
        return jax.random.normal(k, shape, f32) * s

    x = jax.random.normal(ks[0], (BATCH, SEQ, D), f32)
    c = jax.random.normal(ks[1], (BATCH, D), f32)
    w_in = jnp.concatenate([
        nrm(ks[2], (L, D, 3 * ATTN_WIDTH), D ** -0.5),
        nrm(ks[3], (L, D, N_HEADS), 0.5 * D ** -0.5),
        nrm(ks[4], (L, D, 2 * CONV_WIDTH), D ** -0.5),
    ], axis=-1)
    b_f = 2.5 + 0.5 * jax.random.normal(ks[5], (L, N_HEADS), f32)
    conv_w = nrm(ks[6], (L, CONV_K, CONV_WIDTH), CONV_K ** -0.5)
    conv_b = nrm(ks[7], (L, CONV_WIDTH), 0.01)
    conv_ln_g = 1.0 + nrm(ks[8], (L, CONV_WIDTH), 0.05)
    conv_ln_b = nrm(ks[9], (L, CONV_WIDTH), 0.01)
    w_o = nrm(ks[10], (L, D_MIX, D), D_MIX ** -0.5)
    w_ffn_in = nrm(ks[11], (L, D, 2 * D_FF), D ** -0.5)
    w_ffn_out = nrm(ks[12], (L, D_FF, D), D_FF ** -0.5)
    mix_pre_g = 1.0 + nrm(ks[13], (L, D), 0.05)
    mix_post_g = 1.0 + nrm(ks[14], (L, D), 0.05)
    ffn_pre_g = 1.0 + nrm(ks[15], (L, D), 0.05)
    ffn_post_g = 1.0 + nrm(ks[16], (L, D), 0.05)
    ada_w = nrm(ks[17], (L, D, N_MOD * D), 0.5 * D ** -0.5)
    ada_b = nrm(ks[18], (L, N_MOD * D), 0.01)
    return {"x": x, "c": c, "w_in": w_in, "b_f": b_f, "conv_w": conv_w, "conv_b": conv_b,
            "conv_ln_g": conv_ln_g, "conv_ln_b": conv_ln_b, "w_o": w_o,
            "w_ffn_in": w_ffn_in, "w_ffn_out": w_ffn_out,
            "mix_pre_g": mix_pre_g, "mix_post_g": mix_post_g,
            "ffn_pre_g": ffn_pre_g, "ffn_post_g": ffn_post_g,
            "ada_w": ada_w, "ada_b": ada_b}


def reference(x, c, w_in, b_f, conv_w, conv_b, conv_ln_g, conv_ln_b, w_o,
              w_ffn_in, w_ffn_out, mix_pre_g, mix_post_g, ffn_pre_g, ffn_post_g,
              ada_w, ada_b):
    c_act = jax.nn.silu(c)
    for i in range(DEPTH):
        mod = c_act @ ada_w[i] + ada_b[i]
        sh1, sc1, g1, sh2, sc2, g2 = [m[:, None, :] for m in jnp.split(mod, N_MOD, axis=-1)]
        h = rms_norm(x, mix_pre_g[i]) * (1.0 + sc1) + sh1
        y = hybrid_mixer(h, w_in[i], b_f[i], conv_w[i], conv_b[i], conv_ln_g[i], conv_ln_b[i], w_o[i])
        x = x + g1 * rms_norm(y, mix_post_g[i])
        h = rms_norm(x, ffn_pre_g[i]) * (1.0 + sc2) + sh2
        y = swiglu_ffn(h, w_ffn_in[i], w_ffn_out[i])
        x = x + g2 * rms_norm(y, ffn_post_g[i])
    return x
```

```python
import functools
import math

import jax
import jax.numpy as jnp
from jax import lax
from jax.experimental import pallas as pl
from jax.experimental.pallas import tpu as pltpu

D_MODEL = 1024
ATTN_WIDTH = 512
HEAD_DIM = 64
N_HEADS = 8
CONV_WIDTH = 512
CONV_K = 31
D_FF = 2816
N_MOD = 6
EPS = 1e-6

LANES = 128
AUG_STRIDE = 16
NEG_BIG = -1e30

TM_PROJ = 512
TQ = 256
TM_MIX = 512
TM_FFN = 512
FF_CHUNK = 256
HALO = 32
CONV_ROWS = 64
VMEM_LIMIT = 56 * 1024 * 1024


def _split3(x):
    hi = x.astype(jnp.bfloat16).astype(jnp.float32)
    r = x - hi
    mid = r.astype(jnp.bfloat16).astype(jnp.float32)
    lo = (r - mid).astype(jnp.bfloat16).astype(jnp.float32)
    return hi, mid, lo


def _rms(x, g):
    return x * lax.rsqrt(jnp.mean(x * x, axis=-1, keepdims=True) + EPS) * g


def _ada_kernel(c_ref, w_ref, b_ref, o_ref):
    c = c_ref[...]
    ca = (c * jax.nn.sigmoid(c)).astype(jnp.bfloat16)
    w = w_ref[0].astype(jnp.bfloat16)
    o_ref[0] = jnp.dot(ca, w, preferred_element_type=jnp.float32) + b_ref[0]


def _ada_call(c, ada_w, ada_b):
    depth, d, n = ada_w.shape
    b = c.shape[0]
    tn = 1536
    return pl.pallas_call(
        _ada_kernel,
        grid=(depth, n // tn),
        in_specs=[
            pl.BlockSpec((b, d), lambda l, j: (0, 0)),
            pl.BlockSpec((1, d, tn), lambda l, j: (l, 0, j)),
            pl.BlockSpec((1, 1, tn), lambda l, j: (l, 0, j)),
        ],
        out_specs=pl.BlockSpec((1, b, tn), lambda l, j: (l, 0, j)),
        out_shape=jax.ShapeDtypeStruct((depth, b, n), jnp.float32),
        compiler_params=pltpu.CompilerParams(
            dimension_semantics=("arbitrary", "arbitrary"), vmem_limit_bytes=VMEM_LIMIT),
        name="adaln_mod",
    )(c, ada_w, ada_b.reshape(depth, 1, n))


def _proj_kernel(x_ref, mod_ref, g_ref, wqkv_ref, wf_ref, bf_ref, wcv_ref, tri_ref,
                 q_ref, k_ref, v_ref, qa_ref, ka_ref, u_ref, carry_ref, *, tiles_per_seq):
    si = pl.program_id(0) % tiles_per_seq

    @pl.when(si == 0)
    def _():
        carry_ref[...] = jnp.zeros_like(carry_ref)

    x = x_ref[...]
    sh = mod_ref[0, 0:1, :]
    sc = mod_ref[0, 1:2, :]
    h = _rms(x, g_ref[...]) * (1.0 + sc) + sh
    hb = h.astype(jnp.bfloat16)

    aw = ATTN_WIDTH
    q = jnp.dot(hb, wqkv_ref[:, 0:aw], preferred_element_type=jnp.float32)
    q_ref[...] = (q * (HEAD_DIM ** -0.5)).astype(jnp.bfloat16)
    k_ref[...] = jnp.dot(hb, wqkv_ref[:, aw:2 * aw], preferred_element_type=jnp.float32).astype(jnp.bfloat16)
    v_ref[...] = jnp.dot(hb, wqkv_ref[:, 2 * aw:3 * aw], preferred_element_type=jnp.float32).astype(jnp.bfloat16)

    fl = jnp.dot(hb, wf_ref[...], preferred_element_type=jnp.float32) + bf_ref[...]
    logf = -(jnp.maximum(-fl, 0.0) + jnp.log1p(jnp.exp(-jnp.abs(fl))))
    hi, mid, lo = _split3(logf)
    parts = jnp.concatenate([hi, mid, lo], axis=1).astype(jnp.bfloat16)
    cp = jnp.dot(tri_ref[...], parts, preferred_element_type=jnp.float32)
    cum = cp[:, 0:LANES] + cp[:, LANES:2 * LANES] + cp[:, 2 * LANES:3 * LANES] + carry_ref[...]
    tm = cum.shape[0]
    carry_ref[...] = cum[tm - 1:tm, :]

    chi, cmid, clo = _split3(cum)
    one = jnp.ones_like(chi)
    zero = jnp.zeros_like(chi)
    lane = lax.broadcasted_iota(jnp.int32, cum.shape, 1) % AUG_STRIDE
    qa_ref[...] = jnp.where(lane == 0, chi, jnp.where(lane == 1, cmid, jnp.where(
        lane == 2, clo, jnp.where(lane < 6, one, zero)))).astype(qa_ref.dtype)
    ka_ref[...] = jnp.where(lane < 3, one, jnp.where(lane == 3, -chi, jnp.where(
        lane == 4, -cmid, jnp.where(lane == 5, -clo, zero)))).astype(ka_ref.dtype)

    cw = CONV_WIDTH
    val = jnp.dot(hb, wcv_ref[:, 0:cw], preferred_element_type=jnp.float32)
    gate = jnp.dot(hb, wcv_ref[:, cw:2 * cw], preferred_element_type=jnp.float32)
    u_ref[...] = val * jax.nn.sigmoid(gate)


def _proj_call(x2, mod, g_pre, wqkv, wf, bfw, wcv, tri, *, seq):
    t, d = x2.shape
    tm = TM_PROJ
    tps = seq // tm
    row = lambda i: (i, 0)
    const = lambda i: (0, 0)
    bf16 = jnp.bfloat16
    return pl.pallas_call(
        functools.partial(_proj_kernel, tiles_per_seq=tps),
        grid=(t // tm,),
        in_specs=[
            pl.BlockSpec((tm, d), row),
            pl.BlockSpec((1, N_MOD, d), lambda i: (i // tps, 0, 0)),
            pl.BlockSpec((1, d), const),
            pl.BlockSpec(wqkv.shape, const),
            pl.BlockSpec(wf.shape, const),
            pl.BlockSpec(bfw.shape, const),
            pl.BlockSpec(wcv.shape, const),
            pl.BlockSpec(tri.shape, const),
        ],
        out_specs=[
            pl.BlockSpec((tm, ATTN_WIDTH), row),
            pl.BlockSpec((tm, ATTN_WIDTH), row),
            pl.BlockSpec((tm, ATTN_WIDTH), row),
            pl.BlockSpec((tm, LANES), row),
            pl.BlockSpec((tm, LANES), row),
            pl.BlockSpec((tm, CONV_WIDTH), row),
        ],
        out_shape=[
            jax.ShapeDtypeStruct((t, ATTN_WIDTH), bf16),
            jax.ShapeDtypeStruct((t, ATTN_WIDTH), bf16),
            jax.ShapeDtypeStruct((t, ATTN_WIDTH), bf16),
            jax.ShapeDtypeStruct((t, LANES), bf16),
            jax.ShapeDtypeStruct((t, LANES), bf16),
            jax.ShapeDtypeStruct((t, CONV_WIDTH), jnp.float32),
        ],
        scratch_shapes=[pltpu.VMEM((1, LANES), jnp.float32)],
        compiler_params=pltpu.CompilerParams(
            dimension_semantics=("arbitrary",), vmem_limit_bytes=VMEM_LIMIT),
        name="in_proj",
    )(x2, mod, g_pre, wqkv, wf, bfw, wcv, tri)


def _attn_kernel(q_ref, k_ref, v_ref, qa_ref, ka_ref, o_ref, m_ref, acc_ref):
    pair = pl.program_id(1)
    qi = pl.program_id(2)
    tq = q_ref.shape[1]
    bf16 = jnp.bfloat16

    q = q_ref[0].astype(jnp.float32)
    qa = qa_ref[0].astype(jnp.float32)
    lane = lax.broadcasted_iota(jnp.int32, q.shape, 1)
    head_of_lane = lane // AUG_STRIDE
    zq = jnp.zeros_like(q)
    lhs = jnp.concatenate([
        jnp.concatenate([jnp.where(lane < HEAD_DIM, q, zq),
                         jnp.where(head_of_lane == 2 * pair, qa, zq)], axis=1),
        jnp.concatenate([jnp.where(lane >= HEAD_DIM, q, zq),
                         jnp.where(head_of_lane == 2 * pair + 1, qa, zq)], axis=1),
    ], axis=0).astype(bf16)

    m_ref[...] = jnp.full_like(m_ref, NEG_BIG)
    acc_ref[...] = jnp.zeros_like(acc_ref)
    ones = jnp.ones((tq, LANES), bf16)

    def step(kt, masked):
        start = pl.multiple_of(kt * tq, tq)
        kk = jnp.concatenate([k_ref[0, pl.ds(start, tq), :], ka_ref[0, pl.ds(start, tq), :]], axis=1)
        s = lax.dot_general(lhs, kk, (((1,), (1,)), ((), ())), preferred_element_type=jnp.float32)
        if masked:
            r = lax.broadcasted_iota(jnp.int32, s.shape, 0) % tq
            c = lax.broadcasted_iota(jnp.int32, s.shape, 1)
            s = jnp.where(c <= r, s, NEG_BIG)
        m_old = m_ref[...]
        m_new = jnp.maximum(m_old, jnp.max(s, axis=1, keepdims=True))
        alpha = jnp.exp(m_old - m_new)
        p = jnp.exp(s - m_new).astype(bf16)
        vv = jnp.concatenate([v_ref[0, pl.ds(start, tq), :], ones], axis=1)
        acc_ref[...] = acc_ref[...] * alpha + jnp.dot(p, vv, preferred_element_type=jnp.float32)
        m_ref[...] = m_new

    def body(kt, carry):
        step(kt, False)
        return carry

    lax.fori_loop(0, qi, body, 0)
    step(qi, True)

    acc = acc_ref[...]
    denom = acc[:, LANES:LANES + 1]
    o0 = acc[0:tq, 0:LANES] / denom[0:tq]
    o1 = acc[tq:2 * tq, 0:LANES] / denom[tq:2 * tq]
    o_ref[0] = jnp.where(lane < HEAD_DIM, o0, o1).astype(o_ref.dtype)


def _attn_call(q, k, v, qa, ka):
    b, s, w = q.shape
    tq = TQ
    n_pairs = w // LANES
    return pl.pallas_call(
        _attn_kernel,
        grid=(b, n_pairs, s // tq),
        in_specs=[
            pl.BlockSpec((1, tq, LANES), lambda bi, p, i: (bi, i, p)),
            pl.BlockSpec((1, s, LANES), lambda bi, p, i: (bi, 0, p)),
            pl.BlockSpec((1, s, LANES), lambda bi, p, i: (bi, 0, p)),
            pl.BlockSpec((1, tq, LANES), lambda bi, p, i: (bi, i, 0)),
            pl.BlockSpec((1, s, LANES), lambda bi, p, i: (bi, 0, 0)),
        ],
        out_specs=pl.BlockSpec((1, tq, LANES), lambda bi, p, i: (bi, i, p)),
        out_shape=jax.ShapeDtypeStruct((b, s, w), jnp.bfloat16),
        scratch_shapes=[pltpu.VMEM((2 * tq, 1), jnp.float32),
                        pltpu.VMEM((2 * tq, 2 * LANES), jnp.float32)],
        compiler_params=pltpu.CompilerParams(
            dimension_semantics=("arbitrary", "arbitrary", "arbitrary"), vmem_limit_bytes=VMEM_LIMIT),
        name="fox_attention",
    )(q, k, v, qa, ka)


def _mix_kernel(u_ref, halo_ref, attn_ref, x_ref, mod_ref, cw_ref, cb_ref, lg_ref, lb_ref,
                woa_ref, wou_ref, gpost_ref, gfpre_ref, xo_ref, h2_ref, win_ref, act_ref,
                *, tiles_per_seq):
    si = pl.program_id(0) % tiles_per_seq
    tm = u_ref.shape[0]

    halo = halo_ref[...]
    win_ref[0:HALO, :] = jnp.where(si == 0, jnp.zeros_like(halo), halo)
    win_ref[HALO:HALO + tm, :] = u_ref[...]

    off0 = HALO - (CONV_K - 1)
    for r0 in range(0, tm, CONV_ROWS):
        acc = jnp.broadcast_to(cb_ref[...], (CONV_ROWS, CONV_WIDTH))
        for j in range(CONV_K):
            acc = acc + cw_ref[j:j + 1, :] * win_ref[r0 + off0 + j:r0 + off0 + j + CONV_ROWS, :]
        mu = jnp.mean(acc, axis=-1, keepdims=True)
        cen = acc - mu
        var = jnp.mean(cen * cen, axis=-1, keepdims=True)
        y = cen * lax.rsqrt(var + EPS) * lg_ref[...] + lb_ref[...]
        act_ref[r0:r0 + CONV_ROWS, :] = (y * jax.nn.sigmoid(y)).astype(act_ref.dtype)

    y = (jnp.dot(attn_ref[...], woa_ref[...], preferred_element_type=jnp.float32)
         + jnp.dot(act_ref[...], wou_ref[...], preferred_element_type=jnp.float32))
    g1 = mod_ref[0, 2:3, :]
    sh2 = mod_ref[0, 3:4, :]
    sc2 = mod_ref[0, 4:5, :]
    xn = x_ref[...] + g1 * _rms(y, gpost_ref[...])
    xo_ref[...] = xn
    h2_ref[...] = (_rms(xn, gfpre_ref[...]) * (1.0 + sc2) + sh2).astype(h2_ref.dtype)


def _mix_call(u, attn, x2, mod, cw, cb, lg, lb, woa, wou, gpost, gfpre, *, seq):
    t, d = x2.shape
    tm = TM_MIX
    tps = seq // tm
    row = lambda i: (i, 0)
    const = lambda i: (0, 0)
    halo_blocks = tm // HALO
    return pl.pallas_call(
        functools.partial(_mix_kernel, tiles_per_seq=tps),
        grid=(t // tm,),
        in_specs=[
            pl.BlockSpec((tm, CONV_WIDTH), row),
            pl.BlockSpec((HALO, CONV_WIDTH), lambda i: (jnp.maximum(i * halo_blocks - 1, 0), 0)),
            pl.BlockSpec((tm, ATTN_WIDTH), row),
            pl.BlockSpec((tm, d), row),
            pl.BlockSpec((1, N_MOD, d), lambda i: (i // tps, 0, 0)),
            pl.BlockSpec(cw.shape, const),
            pl.BlockSpec(cb.shape, const),
            pl.BlockSpec(lg.shape, const),
            pl.BlockSpec(lb.shape, const),
            pl.BlockSpec(woa.shape, const),
            pl.BlockSpec(wou.shape, const),
            pl.BlockSpec(gpost.shape, const),
            pl.BlockSpec(gfpre.shape, const),
        ],
        out_specs=[pl.BlockSpec((tm, d), row), pl.BlockSpec((tm, d), row)],
        out_shape=[jax.ShapeDtypeStruct((t, d), jnp.float32),
                   jax.ShapeDtypeStruct((t, d), jnp.bfloat16)],
        scratch_shapes=[pltpu.VMEM((HALO + tm, CONV_WIDTH), jnp.float32),
                        pltpu.VMEM((tm, CONV_WIDTH), jnp.bfloat16)],
        compiler_params=pltpu.CompilerParams(
            dimension_semantics=("arbitrary",), vmem_limit_bytes=VMEM_LIMIT),
        name="conv_out_proj",
    )(u, u, attn, x2, mod, cw, cb, lg, lb, woa, wou, gpost, gfpre)


def _ffn_kernel(h_ref, x_ref, mod_ref, win_ref, wout_ref, gpost_ref, xo_ref, a_ref):
    h = h_ref[...]
    for c0 in range(0, D_FF, FF_CHUNK):
        g = jnp.dot(h, win_ref[:, c0:c0 + FF_CHUNK], preferred_element_type=jnp.float32)
        u = jnp.dot(h, win_ref[:, D_FF + c0:D_FF + c0 + FF_CHUNK], preferred_element_type=jnp.float32)
        a_ref[:, c0:c0 + FF_CHUNK] = (g * jax.nn.sigmoid(g) * u).astype(a_ref.dtype)
    y = jnp.dot(a_ref[...], wout_ref[...], preferred_element_type=jnp.float32)
    g2 = mod_ref[0, 5:6, :]
    xo_ref[...] = x_ref[...] + g2 * _rms(y, gpost_ref[...])


def _ffn_call(h2, x2, mod, w_in, w_out, gpost, *, seq):
    t, d = x2.shape
    tm = TM_FFN
    tps = seq // tm
    row = lambda i: (i, 0)
    const = lambda i: (0, 0)
    once = pl.Buffered(1)
    return pl.pallas_call(
        _ffn_kernel,
        grid=(t // tm,),
        in_specs=[
            pl.BlockSpec((tm, d), row),
            pl.BlockSpec((tm, d), row),
            pl.BlockSpec((1, N_MOD, d), lambda i: (i // tps, 0, 0)),
            pl.BlockSpec(w_in.shape, const, pipeline_mode=once),
            pl.BlockSpec(w_out.shape, const, pipeline_mode=once),
            pl.BlockSpec(gpost.shape, const),
        ],
        out_specs=pl.BlockSpec((tm, d), row),
        out_shape=jax.ShapeDtypeStruct((t, d), jnp.float32),
        scratch_shapes=[pltpu.VMEM((tm, D_FF), jnp.bfloat16)],
        compiler_params=pltpu.CompilerParams(
            dimension_semantics=("arbitrary",), vmem_limit_bytes=VMEM_LIMIT),
        name="swiglu_ffn",
    )(h2, x2, mod, w_in, w_out, gpost)


def _forget_weights(w_f, b_f):
    d = w_f.shape[0]
    rep = jnp.repeat(w_f[:, :, None], AUG_STRIDE, axis=2)
    keep = (jnp.arange(AUG_STRIDE) < 6).astype(w_f.dtype)
    wf = (rep * keep).reshape(d, N_HEADS * AUG_STRIDE)
    bf = (jnp.repeat(b_f[:, None], AUG_STRIDE, axis=1) * keep).reshape(1, N_HEADS * AUG_STRIDE)
    return wf, bf


def kernel(x, c, w_in, b_f, conv_w, conv_b, conv_ln_g, conv_ln_b, w_o, w_ffn_in, w_ffn_out,
           mix_pre_g, mix_post_g, ffn_pre_g, ffn_post_g, ada_w, ada_b):
    b, s, d = x.shape
    depth = w_in.shape[0]
    bf16 = jnp.bfloat16
    aw, cw = ATTN_WIDTH, CONV_WIDTH

    mod_all = _ada_call(c, ada_w, ada_b).reshape(depth, b, N_MOD, d)
    tri = jnp.tri(TM_PROJ, dtype=bf16)
    x2 = x.reshape(b * s, d)

    for l in range(depth):
        mod = mod_all[l]
        wqkv = w_in[l, :, 0:3 * aw].astype(bf16)
        wf, bfw = _forget_weights(w_in[l, :, 3 * aw:3 * aw + N_HEADS], b_f[l])
        wcv = w_in[l, :, 3 * aw + N_HEADS:].astype(bf16)
        q, k, v, qa, ka, u = _proj_call(
            x2, mod, mix_pre_g[l].reshape(1, d), wqkv, wf.astype(bf16), bfw, wcv, tri, seq=s)
        attn = _attn_call(q.reshape(b, s, aw), k.reshape(b, s, aw), v.reshape(b, s, aw),
                          qa.reshape(b, s, LANES), ka.reshape(b, s, LANES))
        cwp = jnp.concatenate([conv_w[l], jnp.zeros((1, cw), conv_w.dtype)], axis=0)
        x2, h2 = _mix_call(
            u, attn.reshape(b * s, aw), x2, mod, cwp, conv_b[l].reshape(1, cw),
            conv_ln_g[l].reshape(1, cw), conv_ln_b[l].reshape(1, cw),
            w_o[l, 0:aw, :].astype(bf16), w_o[l, aw:, :].astype(bf16),
            mix_post_g[l].reshape(1, d), ffn_pre_g[l].reshape(1, d), seq=s)
        x2 = _ffn_call(h2, x2, mod, w_ffn_in[l].astype(bf16), w_ffn_out[l].astype(bf16),
                       ffn_post_g[l].reshape(1, d), seq=s)
    return x2.reshape(b, s, d)
```

```python
import functools
import math

import jax
import jax.numpy as jnp
from jax import lax
from jax.experimental import pallas as pl
from jax.experimental.pallas import tpu as pltpu

D_MODEL = 1024
ATTN_WIDTH = 512
HEAD_DIM = 64
N_HEADS = 8
CONV_WIDTH = 512
CONV_K = 31
D_FF = 2816
N_MOD = 6
EPS = 1e-6

LANES = 128
AUG_STRIDE = 16
NEG_BIG = -1e30
LOG2E = math.log2(math.e)

TM_PROJ = 512
TQ = 256
TM_MIX = 512
TM_FFN = 512
FF_CHUNK = 256
HALO = 32
CONV_ROWS = 64
VMEM_LIMIT = 56 * 1024 * 1024


def _split3(x):
    hi = x.astype(jnp.bfloat16).astype(jnp.float32)
    r = x - hi
    mid = r.astype(jnp.bfloat16).astype(jnp.float32)
    lo = (r - mid).astype(jnp.bfloat16).astype(jnp.float32)
    return hi, mid, lo


def _rms(x, g):
    return x * lax.rsqrt(jnp.mean(x * x, axis=-1, keepdims=True) + EPS) * g


def _ada_kernel(c_ref, w_ref, b_ref, o_ref):
    c = c_ref[...]
    ca = (c * jax.nn.sigmoid(c)).astype(jnp.bfloat16)
    w = w_ref[0].astype(jnp.bfloat16)
    o_ref[0] = jnp.dot(ca, w, preferred_element_type=jnp.float32) + b_ref[0]


def _ada_call(c, ada_w, ada_b):
    depth, d, n = ada_w.shape
    b = c.shape[0]
    tn = 1536
    return pl.pallas_call(
        _ada_kernel,
        grid=(depth, n // tn),
        in_specs=[
            pl.BlockSpec((b, d), lambda l, j: (0, 0)),
            pl.BlockSpec((1, d, tn), lambda l, j: (l, 0, j)),
            pl.BlockSpec((1, 1, tn), lambda l, j: (l, 0, j)),
        ],
        out_specs=pl.BlockSpec((1, b, tn), lambda l, j: (l, 0, j)),
        out_shape=jax.ShapeDtypeStruct((depth, b, n), jnp.float32),
        compiler_params=pltpu.CompilerParams(
            dimension_semantics=("arbitrary", "arbitrary"), vmem_limit_bytes=VMEM_LIMIT),
        name="adaln_mod",
    )(c, ada_w, ada_b.reshape(depth, 1, n))


def _proj_kernel(x_ref, mod_ref, g_ref, wqkv_ref, wf_ref, bf_ref, wcv_ref, tri_ref,
                 q_ref, k_ref, v_ref, qa_ref, ka_ref, u_ref, carry_ref, *, tiles_per_seq):
    si = pl.program_id(0) % tiles_per_seq

    @pl.when(si == 0)
    def _():
        carry_ref[...] = jnp.zeros_like(carry_ref)

    x = x_ref[...]
    sh = mod_ref[0, 0:1, :]
    sc = mod_ref[0, 1:2, :]
    h = _rms(x, g_ref[...]) * (1.0 + sc) + sh
    hb = h.astype(jnp.bfloat16)

    aw = ATTN_WIDTH
    q = jnp.dot(hb, wqkv_ref[:, 0:aw], preferred_element_type=jnp.float32)
    q_ref[...] = (q * (HEAD_DIM ** -0.5 * LOG2E)).astype(jnp.bfloat16)
    k_ref[...] = jnp.dot(hb, wqkv_ref[:, aw:2 * aw], preferred_element_type=jnp.float32).astype(jnp.bfloat16)
    v_ref[...] = jnp.dot(hb, wqkv_ref[:, 2 * aw:3 * aw], preferred_element_type=jnp.float32).astype(jnp.bfloat16)

    fl = jnp.dot(hb, wf_ref[...], preferred_element_type=jnp.float32) + bf_ref[...]
    logf = -(jnp.maximum(-fl, 0.0) + jnp.log1p(jnp.exp(-jnp.abs(fl))))
    hi, mid, lo = _split3(logf)
    parts = jnp.concatenate([hi, mid, lo], axis=1).astype(jnp.bfloat16)
    cp = jnp.dot(tri_ref[...], parts, preferred_element_type=jnp.float32)
    cum = cp[:, 0:LANES] + cp[:, LANES:2 * LANES] + cp[:, 2 * LANES:3 * LANES] + carry_ref[...]
    tm = cum.shape[0]
    carry_ref[...] = cum[tm - 1:tm, :]

    chi, cmid, clo = _split3(cum * LOG2E)
    one = jnp.ones_like(chi)
    zero = jnp.zeros_like(chi)
    lane = lax.broadcasted_iota(jnp.int32, cum.shape, 1) % AUG_STRIDE
    qa_ref[...] = jnp.where(lane == 0, chi, jnp.where(lane == 1, cmid, jnp.where(
        lane == 2, clo, jnp.where(lane < 6, one, zero)))).astype(qa_ref.dtype)
    ka_ref[...] = jnp.where(lane < 3, one, jnp.where(lane == 3, -chi, jnp.where(
        lane == 4, -cmid, jnp.where(lane == 5, -clo, zero)))).astype(ka_ref.dtype)

    cw = CONV_WIDTH
    val = jnp.dot(hb, wcv_ref[:, 0:cw], preferred_element_type=jnp.float32)
    gate = jnp.dot(hb, wcv_ref[:, cw:2 * cw], preferred_element_type=jnp.float32)
    u_ref[...] = val * jax.nn.sigmoid(gate)


def _proj_call(x2, mod, g_pre, wqkv, wf, bfw, wcv, tri, *, seq):
    t, d = x2.shape
    tm = TM_PROJ
    tps = seq // tm
    row = lambda i: (i, 0)
    const = lambda i: (0, 0)
    bf16 = jnp.bfloat16
    return pl.pallas_call(
        functools.partial(_proj_kernel, tiles_per_seq=tps),
        grid=(t // tm,),
        in_specs=[
            pl.BlockSpec((tm, d), row),
            pl.BlockSpec((1, N_MOD, d), lambda i: (i // tps, 0, 0)),
            pl.BlockSpec((1, d), const),
            pl.BlockSpec(wqkv.shape, const),
            pl.BlockSpec(wf.shape, const),
            pl.BlockSpec(bfw.shape, const),
            pl.BlockSpec(wcv.shape, const),
            pl.BlockSpec(tri.shape, const),
        ],
        out_specs=[
            pl.BlockSpec((tm, ATTN_WIDTH), row),
            pl.BlockSpec((tm, ATTN_WIDTH), row),
            pl.BlockSpec((tm, ATTN_WIDTH), row),
            pl.BlockSpec((tm, LANES), row),
            pl.BlockSpec((tm, LANES), row),
            pl.BlockSpec((tm, CONV_WIDTH), row),
        ],
        out_shape=[
            jax.ShapeDtypeStruct((t, ATTN_WIDTH), bf16),
            jax.ShapeDtypeStruct((t, ATTN_WIDTH), bf16),
            jax.ShapeDtypeStruct((t, ATTN_WIDTH), bf16),
            jax.ShapeDtypeStruct((t, LANES), bf16),
            jax.ShapeDtypeStruct((t, LANES), bf16),
            jax.ShapeDtypeStruct((t, CONV_WIDTH), jnp.float32),
        ],
        scratch_shapes=[pltpu.VMEM((1, LANES), jnp.float32)],
        compiler_params=pltpu.CompilerParams(
            dimension_semantics=("arbitrary",), vmem_limit_bytes=VMEM_LIMIT),
        name="in_proj",
    )(x2, mod, g_pre, wqkv, wf, bfw, wcv, tri)


def _attn_kernel(q_ref, k_ref, v_ref, qa_ref, ka_ref, o_ref, kk_ref, vv_ref, s_ref, p_ref):
    pair = pl.program_id(1)
    seq = q_ref.shape[1]
    tq = TQ
    bf16 = jnp.bfloat16
    nt = (((1,), (1,)), ((), ()))

    kk_ref[:, 0:LANES] = k_ref[0]
    kk_ref[:, LANES:2 * LANES] = ka_ref[0]
    vv_ref[:, 0:LANES] = v_ref[0]
    vv_ref[:, LANES:2 * LANES] = jnp.ones((seq, LANES), bf16)

    lane = lax.broadcasted_iota(jnp.int32, (tq, LANES), 1)
    head_of_lane = lane // AUG_STRIDE
    row = lax.broadcasted_iota(jnp.int32, (2 * tq, tq), 0) % tq
    col = lax.broadcasted_iota(jnp.int32, (2 * tq, tq), 1)
    causal = col <= row

    for qi in range(seq // tq):
        q = q_ref[0, qi * tq:(qi + 1) * tq, :].astype(jnp.float32)
        qa = qa_ref[0, qi * tq:(qi + 1) * tq, :].astype(jnp.float32)
        zq = jnp.zeros_like(q)
        lhs = jnp.concatenate([
            jnp.concatenate([jnp.where(lane < HEAD_DIM, q, zq),
                             jnp.where(head_of_lane == 2 * pair, qa, zq)], axis=1),
            jnp.concatenate([jnp.where(lane >= HEAD_DIM, q, zq),
                             jnp.where(head_of_lane == 2 * pair + 1, qa, zq)], axis=1),
        ], axis=0).astype(bf16)

        nk = (qi + 1) * tq
        mrun = jnp.full((2 * tq, LANES), NEG_BIG, jnp.float32)
        for c0 in range(0, nk, tq):
            s = lax.dot_general(lhs, kk_ref[c0:c0 + tq, :], nt, preferred_element_type=jnp.float32)
            if c0 + tq == nk:
                s = jnp.where(causal, s, NEG_BIG)
            s_ref[:, c0:c0 + tq] = s
            for l0 in range(0, tq, LANES):
                mrun = jnp.maximum(mrun, s[:, l0:l0 + LANES])
        m = jnp.max(mrun, axis=1, keepdims=True)
        for c0 in range(0, nk, tq):
            p_ref[:, c0:c0 + tq] = jnp.exp2(s_ref[:, c0:c0 + tq] - m).astype(bf16)
        acc = jnp.dot(p_ref[:, 0:nk], vv_ref[0:nk, :], preferred_element_type=jnp.float32)
        denom = acc[:, LANES:LANES + 1]
        o0 = acc[0:tq, 0:LANES] / denom[0:tq]
        o1 = acc[tq:2 * tq, 0:LANES] / denom[tq:2 * tq]
        o_ref[0, qi * tq:(qi + 1) * tq, :] = jnp.where(lane < HEAD_DIM, o0, o1).astype(o_ref.dtype)


def _attn_call(q, k, v, qa, ka):
    b, s, w = q.shape
    n_pairs = w // LANES
    head_blk = lambda bi, p: (bi, 0, p)
    shared_blk = lambda bi, p: (bi, 0, 0)
    return pl.pallas_call(
        _attn_kernel,
        grid=(b, n_pairs),
        in_specs=[
            pl.BlockSpec((1, s, LANES), head_blk),
            pl.BlockSpec((1, s, LANES), head_blk),
            pl.BlockSpec((1, s, LANES), head_blk),
            pl.BlockSpec((1, s, LANES), shared_blk),
            pl.BlockSpec((1, s, LANES), shared_blk),
        ],
        out_specs=pl.BlockSpec((1, s, LANES), head_blk),
        out_shape=jax.ShapeDtypeStruct((b, s, w), jnp.bfloat16),
        scratch_shapes=[pltpu.VMEM((s, 2 * LANES), jnp.bfloat16),
                        pltpu.VMEM((s, 2 * LANES), jnp.bfloat16),
                        pltpu.VMEM((2 * TQ, s), jnp.float32),
                        pltpu.VMEM((2 * TQ, s), jnp.bfloat16)],
        compiler_params=pltpu.CompilerParams(
            dimension_semantics=("arbitrary", "arbitrary"), vmem_limit_bytes=VMEM_LIMIT),
        name="fox_attention",
    )(q, k, v, qa, ka)


def _mix_kernel(u_ref, halo_ref, attn_ref, x_ref, mod_ref, cw_ref, cb_ref, lg_ref, lb_ref,
                woa_ref, wou_ref, gpost_ref, gfpre_ref, xo_ref, h2_ref, win_ref, act_ref,
                *, tiles_per_seq):
    si = pl.program_id(0) % tiles_per_seq
    tm = u_ref.shape[0]

    halo = halo_ref[...]
    win_ref[0:HALO, :] = jnp.where(si == 0, jnp.zeros_like(halo), halo)
    win_ref[HALO:HALO + tm, :] = u_ref[...]

    off0 = HALO - (CONV_K - 1)
    for r0 in range(0, tm, CONV_ROWS):
        acc = jnp.broadcast_to(cb_ref[...], (CONV_ROWS, CONV_WIDTH))
        for j in range(CONV_K):
            acc = acc + cw_ref[j:j + 1, :] * win_ref[r0 + off0 + j:r0 + off0 + j + CONV_ROWS, :]
        mu = jnp.mean(acc, axis=-1, keepdims=True)
        cen = acc - mu
        var = jnp.mean(cen * cen, axis=-1, keepdims=True)
        y = cen * lax.rsqrt(var + EPS) * lg_ref[...] + lb_ref[...]
        act_ref[r0:r0 + CONV_ROWS, :] = (y * jax.nn.sigmoid(y)).astype(act_ref.dtype)

    y = (jnp.dot(attn_ref[...], woa_ref[...], preferred_element_type=jnp.float32)
         + jnp.dot(act_ref[...], wou_ref[...], preferred_element_type=jnp.float32))
    g1 = mod_ref[0, 2:3, :]
    sh2 = mod_ref[0, 3:4, :]
    sc2 = mod_ref[0, 4:5, :]
    xn = x_ref[...] + g1 * _rms(y, gpost_ref[...])
    xo_ref[...] = xn
    h2_ref[...] = (_rms(xn, gfpre_ref[...]) * (1.0 + sc2) + sh2).astype(h2_ref.dtype)


def _mix_call(u, attn, x2, mod, cw, cb, lg, lb, woa, wou, gpost, gfpre, *, seq):
    t, d = x2.shape
    tm = TM_MIX
    tps = seq // tm
    row = lambda i: (i, 0)
    const = lambda i: (0, 0)
    halo_blocks = tm // HALO
    return pl.pallas_call(
        functools.partial(_mix_kernel, tiles_per_seq=tps),
        grid=(t // tm,),
        in_specs=[
            pl.BlockSpec((tm, CONV_WIDTH), row),
            pl.BlockSpec((HALO, CONV_WIDTH), lambda i: (jnp.maximum(i * halo_blocks - 1, 0), 0)),
            pl.BlockSpec((tm, ATTN_WIDTH), row),
            pl.BlockSpec((tm, d), row),
            pl.BlockSpec((1, N_MOD, d), lambda i: (i // tps, 0, 0)),
            pl.BlockSpec(cw.shape, const),
            pl.BlockSpec(cb.shape, const),
            pl.BlockSpec(lg.shape, const),
            pl.BlockSpec(lb.shape, const),
            pl.BlockSpec(woa.shape, const),
            pl.BlockSpec(wou.shape, const),
            pl.BlockSpec(gpost.shape, const),
            pl.BlockSpec(gfpre.shape, const),
        ],
        out_specs=[pl.BlockSpec((tm, d), row), pl.BlockSpec((tm, d), row)],
        out_shape=[jax.ShapeDtypeStruct((t, d), jnp.float32),
                   jax.ShapeDtypeStruct((t, d), jnp.bfloat16)],
        scratch_shapes=[pltpu.VMEM((HALO + tm, CONV_WIDTH), jnp.float32),
                        pltpu.VMEM((tm, CONV_WIDTH), jnp.bfloat16)],
        compiler_params=pltpu.CompilerParams(
            dimension_semantics=("arbitrary",), vmem_limit_bytes=VMEM_LIMIT),
        name="conv_out_proj",
    )(u, u, attn, x2, mod, cw, cb, lg, lb, woa, wou, gpost, gfpre)


def _ffn_kernel(h_ref, x_ref, mod_ref, win_ref, wout_ref, gpost_ref, xo_ref, a_ref):
    h = h_ref[...]
    for c0 in range(0, D_FF, FF_CHUNK):
        g = jnp.dot(h, win_ref[:, c0:c0 + FF_CHUNK], preferred_element_type=jnp.float32)
        u = jnp.dot(h, win_ref[:, D_FF + c0:D_FF + c0 + FF_CHUNK], preferred_element_type=jnp.float32)
        a_ref[:, c0:c0 + FF_CHUNK] = (g * jax.nn.sigmoid(g) * u).astype(a_ref.dtype)
    y = jnp.dot(a_ref[...], wout_ref[...], preferred_element_type=jnp.float32)
    g2 = mod_ref[0, 5:6, :]
    xo_ref[...] = x_ref[...] + g2 * _rms(y, gpost_ref[...])


def _ffn_call(h2, x2, mod, w_in, w_out, gpost, *, seq):
    t, d = x2.shape
    tm = TM_FFN
    tps = seq // tm
    row = lambda i: (i, 0)
    const = lambda i: (0, 0)
    once = pl.Buffered(1)
    return pl.pallas_call(
        _ffn_kernel,
        grid=(t // tm,),
        in_specs=[
            pl.BlockSpec((tm, d), row),
            pl.BlockSpec((tm, d), row),
            pl.BlockSpec((1, N_MOD, d), lambda i: (i // tps, 0, 0)),
            pl.BlockSpec(w_in.shape, const, pipeline_mode=once),
            pl.BlockSpec(w_out.shape, const, pipeline_mode=once),
            pl.BlockSpec(gpost.shape, const),
        ],
        out_specs=pl.BlockSpec((tm, d), row),
        out_shape=jax.ShapeDtypeStruct((t, d), jnp.float32),
        scratch_shapes=[pltpu.VMEM((tm, D_FF), jnp.bfloat16)],
        compiler_params=pltpu.CompilerParams(
            dimension_semantics=("arbitrary",), vmem_limit_bytes=VMEM_LIMIT),
        name="swiglu_ffn",
    )(h2, x2, mod, w_in, w_out, gpost)


def _forget_weights(w_f, b_f):
    d = w_f.shape[0]
    rep = jnp.repeat(w_f[:, :, None], AUG_STRIDE, axis=2)
    keep = (jnp.arange(AUG_STRIDE) < 6).astype(w_f.dtype)
    wf = (rep * keep).reshape(d, N_HEADS * AUG_STRIDE)
    bf = (jnp.repeat(b_f[:, None], AUG_STRIDE, axis=1) * keep).reshape(1, N_HEADS * AUG_STRIDE)
    return wf, bf


def kernel(x, c, w_in, b_f, conv_w, conv_b, conv_ln_g, conv_ln_b, w_o, w_ffn_in, w_ffn_out,
           mix_pre_g, mix_post_g, ffn_pre_g, ffn_post_g, ada_w, ada_b):
    b, s, d = x.shape
    depth = w_in.shape[0]
    bf16 = jnp.bfloat16
    aw, cw = ATTN_WIDTH, CONV_WIDTH

    mod_all = _ada_call(c, ada_w, ada_b).reshape(depth, b, N_MOD, d)
    tri = jnp.tri(TM_PROJ, dtype=bf16)
    x2 = x.reshape(b * s, d)

    for l in range(depth):
        mod = mod_all[l]
        wqkv = w_in[l, :, 0:3 * aw].astype(bf16)
        wf, bfw = _forget_weights(w_in[l, :, 3 * aw:3 * aw + N_HEADS], b_f[l])
        wcv = w_in[l, :, 3 * aw + N_HEADS:].astype(bf16)
        q, k, v, qa, ka, u = _proj_call(
            x2, mod, mix_pre_g[l].reshape(1, d), wqkv, wf.astype(bf16), bfw, wcv, tri, seq=s)
        attn = _attn_call(q.reshape(b, s, aw), k.reshape(b, s, aw), v.reshape(b, s, aw),
                          qa.reshape(b, s, LANES), ka.reshape(b, s, LANES))
        cwp = jnp.concatenate([conv_w[l], jnp.zeros((1, cw), conv_w.dtype)], axis=0)
        x2, h2 = _mix_call(
            u, attn.reshape(b * s, aw), x2, mod, cwp, conv_b[l].reshape(1, cw),
            conv_ln_g[l].reshape(1, cw), conv_ln_b[l].reshape(1, cw),
            w_o[l, 0:aw, :].astype(bf16), w_o[l, aw:, :].astype(bf16),
            mix_post_g[l].reshape(1, d), ffn_pre_g[l].reshape(1, d), seq=s)
        x2 = _ffn_call(h2, x2, mod, w_ffn_in[l].astype(bf16), w_ffn_out[l].astype(bf16),
                       ffn_post_g[l].reshape(1, d), seq=s)
    return x2.reshape(b, s, d)
```

```python
import functools
import math

import jax
import jax.numpy as jnp
from jax import lax
from jax.experimental import pallas as pl
from jax.experimental.pallas import tpu as pltpu

D_MODEL = 1024
ATTN_WIDTH = 512
HEAD_DIM = 64
N_HEADS = 8
CONV_WIDTH = 512
CONV_K = 31
D_FF = 2816
N_MOD = 6
EPS = 1e-6

LANES = 128
SUBLANES = 8
AUG_STRIDE = 16
NEG_BIG = -1e30
LOG2E = math.log2(math.e)

TM_PROJ = 512
TQ = 256
TM_FFN = 512
FF_CHUNK = 256
HALO_TILES = 4
CONV_TILES = 8
VMEM_LIMIT = 56 * 1024 * 1024


def _split3(x):
    hi = x.astype(jnp.bfloat16).astype(jnp.float32)
    r = x - hi
    mid = r.astype(jnp.bfloat16).astype(jnp.float32)
    lo = (r - mid).astype(jnp.bfloat16).astype(jnp.float32)
    return hi, mid, lo


def _rms(x, g):
    return x * lax.rsqrt(jnp.mean(x * x, axis=-1, keepdims=True) + EPS) * g


def _ada_kernel(c_ref, w_ref, b_ref, o_ref):
    c = c_ref[...]
    ca = (c * jax.nn.sigmoid(c)).astype(jnp.bfloat16)
    w = w_ref[0].astype(jnp.bfloat16)
    o_ref[0] = jnp.dot(ca, w, preferred_element_type=jnp.float32) + b_ref[0]


def _ada_call(c, ada_w, ada_b):
    depth, d, n = ada_w.shape
    b = c.shape[0]
    tn = 1536
    return pl.pallas_call(
        _ada_kernel,
        grid=(depth, n // tn),
        in_specs=[
            pl.BlockSpec((b, d), lambda l, j: (0, 0)),
            pl.BlockSpec((1, d, tn), lambda l, j: (l, 0, j)),
            pl.BlockSpec((1, 1, tn), lambda l, j: (l, 0, j)),
        ],
        out_specs=pl.BlockSpec((1, b, tn), lambda l, j: (l, 0, j)),
        out_shape=jax.ShapeDtypeStruct((depth, b, n), jnp.float32),
        compiler_params=pltpu.CompilerParams(
            dimension_semantics=("arbitrary", "arbitrary"), vmem_limit_bytes=VMEM_LIMIT),
        name="adaln_mod",
    )(c, ada_w, ada_b.reshape(depth, 1, n))


def _proj_kernel(x_ref, mod_ref, g_ref, wqkv_ref, wf_ref, bf_ref, wcv_ref, tri_ref,
                 q_ref, k_ref, v_ref, qa_ref, ka_ref, u_ref, carry_ref, *, tiles_per_seq):
    si = pl.program_id(0) % tiles_per_seq

    @pl.when(si == 0)
    def _():
        carry_ref[...] = jnp.zeros_like(carry_ref)

    x = x_ref[...]
    sh = mod_ref[0, 0:1, :]
    sc = mod_ref[0, 1:2, :]
    h = _rms(x, g_ref[...]) * (1.0 + sc) + sh
    hb = h.astype(jnp.bfloat16)

    aw = ATTN_WIDTH
    q = jnp.dot(hb, wqkv_ref[:, 0:aw], preferred_element_type=jnp.float32)
    q_ref[...] = (q * (HEAD_DIM ** -0.5 * LOG2E)).astype(jnp.bfloat16)
    k_ref[...] = jnp.dot(hb, wqkv_ref[:, aw:2 * aw], preferred_element_type=jnp.float32).astype(jnp.bfloat16)
    v_ref[...] = jnp.dot(hb, wqkv_ref[:, 2 * aw:3 * aw], preferred_element_type=jnp.float32).astype(jnp.bfloat16)

    fl = jnp.dot(hb, wf_ref[...], preferred_element_type=jnp.float32) + bf_ref[...]
    logf = -(jnp.maximum(-fl, 0.0) + jnp.log1p(jnp.exp(-jnp.abs(fl))))
    hi, mid, lo = _split3(logf)
    parts = jnp.concatenate([hi, mid, lo], axis=1).astype(jnp.bfloat16)
    cp = jnp.dot(tri_ref[...], parts, preferred_element_type=jnp.float32)
    cum = cp[:, 0:LANES] + cp[:, LANES:2 * LANES] + cp[:, 2 * LANES:3 * LANES] + carry_ref[...]
    tm = cum.shape[0]
    carry_ref[...] = cum[tm - 1:tm, :]

    chi, cmid, clo = _split3(cum * LOG2E)
    one = jnp.ones_like(chi)
    zero = jnp.zeros_like(chi)
    lane = lax.broadcasted_iota(jnp.int32, cum.shape, 1) % AUG_STRIDE
    qa_ref[...] = jnp.where(lane == 0, chi, jnp.where(lane == 1, cmid, jnp.where(
        lane == 2, clo, jnp.where(lane < 6, one, zero)))).astype(qa_ref.dtype)
    ka_ref[...] = jnp.where(lane < 3, one, jnp.where(lane == 3, -chi, jnp.where(
        lane == 4, -cmid, jnp.where(lane == 5, -clo, zero)))).astype(ka_ref.dtype)

    cw = CONV_WIDTH
    val = jnp.dot(hb, wcv_ref[:, 0:cw], preferred_element_type=jnp.float32)
    gate = jnp.dot(hb, wcv_ref[:, cw:2 * cw], preferred_element_type=jnp.float32)
    u_ref[...] = val * jax.nn.sigmoid(gate)


def _proj_call(x2, mod, g_pre, wqkv, wf, bfw, wcv, tri, *, seq):
    t, d = x2.shape
    tm = TM_PROJ
    tps = seq // tm
    row = lambda i: (i, 0)
    const = lambda i: (0, 0)
    bf16 = jnp.bfloat16
    return pl.pallas_call(
        functools.partial(_proj_kernel, tiles_per_seq=tps),
        grid=(t // tm,),
        in_specs=[
            pl.BlockSpec((tm, d), row),
            pl.BlockSpec((1, N_MOD, d), lambda i: (i // tps, 0, 0)),
            pl.BlockSpec((1, d), const),
            pl.BlockSpec(wqkv.shape, const),
            pl.BlockSpec(wf.shape, const),
            pl.BlockSpec(bfw.shape, const),
            pl.BlockSpec(wcv.shape, const),
            pl.BlockSpec(tri.shape, const),
        ],
        out_specs=[
            pl.BlockSpec((tm, ATTN_WIDTH), row),
            pl.BlockSpec((tm, ATTN_WIDTH), row),
            pl.BlockSpec((tm, ATTN_WIDTH), row),
            pl.BlockSpec((tm, LANES), row),
            pl.BlockSpec((tm, LANES), row),
            pl.BlockSpec((tm, CONV_WIDTH), row),
        ],
        out_shape=[
            jax.ShapeDtypeStruct((t, ATTN_WIDTH), bf16),
            jax.ShapeDtypeStruct((t, ATTN_WIDTH), bf16),
            jax.ShapeDtypeStruct((t, ATTN_WIDTH), bf16),
            jax.ShapeDtypeStruct((t, LANES), bf16),
            jax.ShapeDtypeStruct((t, LANES), bf16),
            jax.ShapeDtypeStruct((t, CONV_WIDTH), jnp.float32),
        ],
        scratch_shapes=[pltpu.VMEM((1, LANES), jnp.float32)],
        compiler_params=pltpu.CompilerParams(
            dimension_semantics=("arbitrary",), vmem_limit_bytes=VMEM_LIMIT),
        name="in_proj",
    )(x2, mod, g_pre, wqkv, wf, bfw, wcv, tri)


def _conv_chunk(rot_ref, cw_ref, cb_ref, cv_ref, i0, built):
    n = CONV_TILES
    off0 = SUBLANES * HALO_TILES - (CONV_K - 1)
    need = i0 + n + max((j + off0) // SUBLANES for j in range(CONV_K) if (j + off0) % SUBLANES)
    if need > built:
        nb = need - built
        sub = lax.broadcasted_iota(jnp.int32, (nb, SUBLANES, LANES), 1)
        w = rot_ref[0, built:built + nb + 1]
        for rho in range(1, SUBLANES):
            wr = pltpu.roll(w, SUBLANES - rho, axis=1)
            rot_ref[rho, built:built + nb] = jnp.where(sub < SUBLANES - rho, wr[0:nb], wr[1:nb + 1])
        built = need
    acc = jnp.broadcast_to(cb_ref[...], (n, SUBLANES, LANES))
    for j in range(CONV_K):
        rho, m = (j + off0) % SUBLANES, (j + off0) // SUBLANES
        acc = acc + cw_ref[j] * rot_ref[rho, i0 + m:i0 + m + n]
    cv_ref[0, i0:i0 + n] = acc
    return built


def _attn_kernel(q_ref, k_ref, v_ref, qa_ref, ka_ref, u_ref, cw_ref, cb_ref, o_ref, cv_ref,
                 kk_ref, vv_ref, s_ref, p_ref, rot_ref):
    pair = pl.program_id(1)
    seq = q_ref.shape[1]
    tq = TQ
    bf16 = jnp.bfloat16
    nt = (((1,), (1,)), ((), ()))

    kk_ref[:, 0:LANES] = k_ref[0]
    kk_ref[:, LANES:2 * LANES] = ka_ref[0]
    vv_ref[:, 0:LANES] = v_ref[0]
    vv_ref[:, LANES:2 * LANES] = jnp.ones((seq, LANES), bf16)
    rot_ref[0, 0:HALO_TILES] = jnp.zeros((HALO_TILES, SUBLANES, LANES), jnp.float32)
    rot_ref[0, HALO_TILES:] = u_ref[0]

    lane = lax.broadcasted_iota(jnp.int32, (tq, LANES), 1)
    head_of_lane = lane // AUG_STRIDE
    row = lax.broadcasted_iota(jnp.int32, (2 * tq, tq), 0) % tq
    col = lax.broadcasted_iota(jnp.int32, (2 * tq, tq), 1)
    causal = col <= row

    n_conv = seq // SUBLANES // CONV_TILES
    conv_done = 0
    built = 0
    n_chunks_total = sum(range(1, seq // tq + 1))
    chunk_no = 0

    for qi in range(seq // tq):
        q = q_ref[0, qi * tq:(qi + 1) * tq, :].astype(jnp.float32)
        qa = qa_ref[0, qi * tq:(qi + 1) * tq, :].astype(jnp.float32)
        zq = jnp.zeros_like(q)
        lhs = jnp.concatenate([
            jnp.concatenate([jnp.where(lane < HEAD_DIM, q, zq),
                             jnp.where(head_of_lane == 2 * pair, qa, zq)], axis=1),
            jnp.concatenate([jnp.where(lane >= HEAD_DIM, q, zq),
                             jnp.where(head_of_lane == 2 * pair + 1, qa, zq)], axis=1),
        ], axis=0).astype(bf16)

        nk = (qi + 1) * tq
        mrun = jnp.full((2 * tq, LANES), NEG_BIG, jnp.float32)
        for c0 in range(0, nk, tq):
            s = lax.dot_general(lhs, kk_ref[c0:c0 + tq, :], nt, preferred_element_type=jnp.float32)
            if c0 + tq == nk:
                s = jnp.where(causal, s, NEG_BIG)
            s_ref[:, c0:c0 + tq] = s
            for l0 in range(0, tq, LANES):
                mrun = jnp.maximum(mrun, s[:, l0:l0 + LANES])
            chunk_no += 1
            while conv_done * n_chunks_total < chunk_no * n_conv:
                built = _conv_chunk(rot_ref, cw_ref, cb_ref, cv_ref, conv_done * CONV_TILES, built)
                conv_done += 1
        m = jnp.max(mrun, axis=1, keepdims=True)
        for c0 in range(0, nk, tq):
            p_ref[:, c0:c0 + tq] = jnp.exp2(s_ref[:, c0:c0 + tq] - m).astype(bf16)
        acc = jnp.dot(p_ref[:, 0:nk], vv_ref[0:nk, :], preferred_element_type=jnp.float32)
        denom = acc[:, LANES:LANES + 1]
        o0 = acc[0:tq, 0:LANES] / denom[0:tq]
        o1 = acc[tq:2 * tq, 0:LANES] / denom[tq:2 * tq]
        o_ref[0, qi * tq:(qi + 1) * tq, :] = jnp.where(lane < HEAD_DIM, o0, o1).astype(o_ref.dtype)


def _attn_call(q, k, v, qa, ka, u4, cwb, cbb):
    b, s, w = q.shape
    n_pairs = w // LANES
    tiles = s // SUBLANES
    head_blk = lambda bi, p: (bi, 0, p)
    shared_blk = lambda bi, p: (bi, 0, 0)
    return pl.pallas_call(
        _attn_kernel,
        grid=(b, n_pairs),
        in_specs=[
            pl.BlockSpec((1, s, LANES), head_blk),
            pl.BlockSpec((1, s, LANES), head_blk),
            pl.BlockSpec((1, s, LANES), head_blk),
            pl.BlockSpec((1, s, LANES), shared_blk),
            pl.BlockSpec((1, s, LANES), shared_blk),
            pl.BlockSpec((1, tiles, SUBLANES, LANES), lambda bi, p: (bi, 0, 0, p)),
            pl.BlockSpec((CONV_K, SUBLANES, LANES), lambda bi, p: (0, 0, p)),
            pl.BlockSpec((SUBLANES, LANES), lambda bi, p: (0, p)),
        ],
        out_specs=[
            pl.BlockSpec((1, s, LANES), head_blk),
            pl.BlockSpec((1, tiles, SUBLANES, LANES), lambda bi, p: (bi, 0, 0, p)),
        ],
        out_shape=[
            jax.ShapeDtypeStruct((b, s, w), jnp.bfloat16),
            jax.ShapeDtypeStruct(u4.shape, jnp.float32),
        ],
        scratch_shapes=[pltpu.VMEM((s, 2 * LANES), jnp.bfloat16),
                        pltpu.VMEM((s, 2 * LANES), jnp.bfloat16),
                        pltpu.VMEM((2 * TQ, s), jnp.float32),
                        pltpu.VMEM((2 * TQ, s), jnp.bfloat16),
                        pltpu.VMEM((SUBLANES, HALO_TILES + tiles, SUBLANES, LANES), jnp.float32)],
        compiler_params=pltpu.CompilerParams(
            dimension_semantics=("arbitrary", "arbitrary"), vmem_limit_bytes=VMEM_LIMIT),
        name="fox_attention_conv",
    )(q, k, v, qa, ka, u4, cwb, cbb)


def _mixffn_kernel(cv_ref, attn_ref, x_ref, mod_ref, lg_ref, lb_ref, woa_ref, wou_ref,
                   gpost_ref, gfpre_ref, win_ref, wout_ref, gfpost_ref, xo_ref, a_ref):
    bf16 = jnp.bfloat16
    cv = cv_ref[...]
    mu = jnp.mean(cv, axis=-1, keepdims=True)
    cen = cv - mu
    var = jnp.mean(cen * cen, axis=-1, keepdims=True)
    yl = cen * lax.rsqrt(var + EPS) * lg_ref[...] + lb_ref[...]
    act = (yl * jax.nn.sigmoid(yl)).astype(bf16)

    y = (jnp.dot(attn_ref[...], woa_ref[...], preferred_element_type=jnp.float32)
         + jnp.dot(act, wou_ref[...], preferred_element_type=jnp.float32))
    g1 = mod_ref[0, 2:3, :]
    sh2 = mod_ref[0, 3:4, :]
    sc2 = mod_ref[0, 4:5, :]
    g2 = mod_ref[0, 5:6, :]
    xn = x_ref[...] + g1 * _rms(y, gpost_ref[...])
    h = (_rms(xn, gfpre_ref[...]) * (1.0 + sc2) + sh2).astype(bf16)

    for c0 in range(0, D_FF, FF_CHUNK):
        g = jnp.dot(h, win_ref[:, c0:c0 + FF_CHUNK], preferred_element_type=jnp.float32)
        u = jnp.dot(h, win_ref[:, D_FF + c0:D_FF + c0 + FF_CHUNK], preferred_element_type=jnp.float32)
        a_ref[:, c0:c0 + FF_CHUNK] = (g * jax.nn.sigmoid(g) * u).astype(bf16)
    y2 = jnp.dot(a_ref[...], wout_ref[...], preferred_element_type=jnp.float32)
    xo_ref[...] = xn + g2 * _rms(y2, gfpost_ref[...])


def _mixffn_call(cv, attn, x2, mod, lg, lb, woa, wou, gpost, gfpre, w_in, w_out, gfpost, *, seq):
    t, d = x2.shape
    tm = TM_FFN
    tps = seq // tm
    row = lambda i: (i, 0)
    const = lambda i: (0, 0)
    once = pl.Buffered(1)
    resident = lambda a: pl.BlockSpec(a.shape, const, pipeline_mode=once)
    return pl.pallas_call(
        _mixffn_kernel,
        grid=(t // tm,),
        in_specs=[
            pl.BlockSpec((tm, CONV_WIDTH), row),
            pl.BlockSpec((tm, ATTN_WIDTH), row),
            pl.BlockSpec((tm, d), row),
            pl.BlockSpec((1, N_MOD, d), lambda i: (i // tps, 0, 0)),
            resident(lg), resident(lb), resident(woa), resident(wou),
            resident(gpost), resident(gfpre), resident(w_in), resident(w_out), resident(gfpost),
        ],
        out_specs=pl.BlockSpec((tm, d), row),
        out_shape=jax.ShapeDtypeStruct((t, d), jnp.float32),
        scratch_shapes=[pltpu.VMEM((tm, D_FF), jnp.bfloat16)],
        compiler_params=pltpu.CompilerParams(
            dimension_semantics=("arbitrary",), vmem_limit_bytes=VMEM_LIMIT),
        name="outproj_ffn",
    )(cv, attn, x2, mod, lg, lb, woa, wou, gpost, gfpre, w_in, w_out, gfpost)


def _forget_weights(w_f, b_f):
    d = w_f.shape[0]
    rep = jnp.repeat(w_f[:, :, None], AUG_STRIDE, axis=2)
    keep = (jnp.arange(AUG_STRIDE) < 6).astype(w_f.dtype)
    wf = (rep * keep).reshape(d, N_HEADS * AUG_STRIDE)
    bf = (jnp.repeat(b_f[:, None], AUG_STRIDE, axis=1) * keep).reshape(1, N_HEADS * AUG_STRIDE)
    return wf, bf


def kernel(x, c, w_in, b_f, conv_w, conv_b, conv_ln_g, conv_ln_b, w_o, w_ffn_in, w_ffn_out,
           mix_pre_g, mix_post_g, ffn_pre_g, ffn_post_g, ada_w, ada_b):
    b, s, d = x.shape
    depth = w_in.shape[0]
    bf16 = jnp.bfloat16
    aw, cw = ATTN_WIDTH, CONV_WIDTH

    mod_all = _ada_call(c, ada_w, ada_b).reshape(depth, b, N_MOD, d)
    tri = jnp.tri(TM_PROJ, dtype=bf16)
    x2 = x.reshape(b * s, d)

    for l in range(depth):
        mod = mod_all[l]
        wqkv = w_in[l, :, 0:3 * aw].astype(bf16)
        wf, bfw = _forget_weights(w_in[l, :, 3 * aw:3 * aw + N_HEADS], b_f[l])
        wcv = w_in[l, :, 3 * aw + N_HEADS:].astype(bf16)
        q, k, v, qa, ka, u = _proj_call(
            x2, mod, mix_pre_g[l].reshape(1, d), wqkv, wf.astype(bf16), bfw, wcv, tri, seq=s)
        cwb = jnp.broadcast_to(conv_w[l][:, None, :], (CONV_K, SUBLANES, cw))
        cbb = jnp.broadcast_to(conv_b[l][None, :], (SUBLANES, cw))
        attn, cv = _attn_call(
            q.reshape(b, s, aw), k.reshape(b, s, aw), v.reshape(b, s, aw),
            qa.reshape(b, s, LANES), ka.reshape(b, s, LANES),
            u.reshape(b, s // SUBLANES, SUBLANES, cw), cwb, cbb)
        x2 = _mixffn_call(
            cv.reshape(b * s, cw), attn.reshape(b * s, aw), x2, mod,
            conv_ln_g[l].reshape(1, cw), conv_ln_b[l].reshape(1, cw),
            w_o[l, 0:aw, :].astype(bf16), w_o[l, aw:, :].astype(bf16),
            mix_post_g[l].reshape(1, d), ffn_pre_g[l].reshape(1, d),
            w_ffn_in[l].astype(bf16), w_ffn_out[l].astype(bf16),
            ffn_post_g[l].reshape(1, d), seq=s)
    return x2.reshape(b, s, d)
```

```python
import functools
import math

import jax
import jax.numpy as jnp
from jax import lax
from jax.experimental import pallas as pl
from jax.experimental.pallas import tpu as pltpu

D_MODEL = 1024
ATTN_WIDTH = 512
HEAD_DIM = 64
N_HEADS = 8
CONV_WIDTH = 512
CONV_K = 31
D_FF = 2816
N_MOD = 6
EPS = 1e-6

LANES = 128
SUBLANES = 8
AUG_STRIDE = 16
NEG_BIG = -1e30
LOG2E = math.log2(math.e)

TM_PROJ = 512
TQ = 256
TM_FFN = 512
FF_CHUNK = 256
HALO_TILES = 4
CONV_TILES = 8
VMEM_LIMIT = 56 * 1024 * 1024


def _split3(x):
    hi = x.astype(jnp.bfloat16).astype(jnp.float32)
    r = x - hi
    mid = r.astype(jnp.bfloat16).astype(jnp.float32)
    lo = (r - mid).astype(jnp.bfloat16).astype(jnp.float32)
    return hi, mid, lo


def _rms(x, g):
    return x * lax.rsqrt(jnp.mean(x * x, axis=-1, keepdims=True) + EPS) * g


def _ada_kernel(c_ref, w_ref, b_ref, o_ref):
    c = c_ref[...]
    ca = (c * jax.nn.sigmoid(c)).astype(jnp.bfloat16)
    w = w_ref[0].astype(jnp.bfloat16)
    o_ref[0] = jnp.dot(ca, w, preferred_element_type=jnp.float32) + b_ref[0]


def _ada_call(c, ada_w, ada_b):
    depth, d, n = ada_w.shape
    b = c.shape[0]
    tn = 1536
    return pl.pallas_call(
        _ada_kernel,
        grid=(depth, n // tn),
        in_specs=[
            pl.BlockSpec((b, d), lambda l, j: (0, 0)),
            pl.BlockSpec((1, d, tn), lambda l, j: (l, 0, j)),
            pl.BlockSpec((1, 1, tn), lambda l, j: (l, 0, j)),
        ],
        out_specs=pl.BlockSpec((1, b, tn), lambda l, j: (l, 0, j)),
        out_shape=jax.ShapeDtypeStruct((depth, b, n), jnp.float32),
        compiler_params=pltpu.CompilerParams(
            dimension_semantics=("arbitrary", "arbitrary"), vmem_limit_bytes=VMEM_LIMIT),
        name="adaln_mod",
    )(c, ada_w, ada_b.reshape(depth, 1, n))


def _proj_kernel(x_ref, mod_ref, g_ref, wqkv_ref, wf_ref, bf_ref, wcv_ref, tri_ref,
                 q_ref, k_ref, v_ref, qa_ref, ka_ref, u_ref, carry_ref, *, tiles_per_seq):
    si = pl.program_id(0) % tiles_per_seq

    @pl.when(si == 0)
    def _():
        carry_ref[...] = jnp.zeros_like(carry_ref)

    x = x_ref[...]
    sh = mod_ref[0, 0:1, :]
    sc = mod_ref[0, 1:2, :]
    h = _rms(x, g_ref[...]) * (1.0 + sc) + sh
    hb = h.astype(jnp.bfloat16)

    aw = ATTN_WIDTH
    q = jnp.dot(hb, wqkv_ref[:, 0:aw], preferred_element_type=jnp.float32)
    q_ref[...] = (q * (HEAD_DIM ** -0.5 * LOG2E)).astype(jnp.bfloat16)
    k_ref[...] = jnp.dot(hb, wqkv_ref[:, aw:2 * aw], preferred_element_type=jnp.float32).astype(jnp.bfloat16)
    v_ref[...] = jnp.dot(hb, wqkv_ref[:, 2 * aw:3 * aw], preferred_element_type=jnp.float32).astype(jnp.bfloat16)

    fl = jnp.dot(hb, wf_ref[...], preferred_element_type=jnp.float32) + bf_ref[...]
    logf = -(jnp.maximum(-fl, 0.0) + jnp.log1p(jnp.exp(-jnp.abs(fl))))
    hi, mid, lo = _split3(logf)
    parts = jnp.concatenate([hi, mid, lo], axis=1).astype(jnp.bfloat16)
    cp = jnp.dot(tri_ref[...], parts, preferred_element_type=jnp.float32)
    cum = cp[:, 0:LANES] + cp[:, LANES:2 * LANES] + cp[:, 2 * LANES:3 * LANES] + carry_ref[...]
    tm = cum.shape[0]
    carry_ref[...] = cum[tm - 1:tm, :]

    chi, cmid, clo = _split3(cum * LOG2E)
    one = jnp.ones_like(chi)
    zero = jnp.zeros_like(chi)
    lane = lax.broadcasted_iota(jnp.int32, cum.shape, 1) % AUG_STRIDE
    qa_ref[...] = jnp.where(lane == 0, chi, jnp.where(lane == 1, cmid, jnp.where(
        lane == 2, clo, jnp.where(lane < 6, one, zero)))).astype(qa_ref.dtype)
    ka_ref[...] = jnp.where(lane < 3, one, jnp.where(lane == 3, -chi, jnp.where(
        lane == 4, -cmid, jnp.where(lane == 5, -clo, zero)))).astype(ka_ref.dtype)

    cw = CONV_WIDTH
    val = jnp.dot(hb, wcv_ref[:, 0:cw], preferred_element_type=jnp.float32)
    gate = jnp.dot(hb, wcv_ref[:, cw:2 * cw], preferred_element_type=jnp.float32)
    u_ref[...] = val * jax.nn.sigmoid(gate)


def _proj_call(x2, mod, g_pre, wqkv, wf, bfw, wcv, tri, *, layer, seq):
    t, d = x2.shape
    tm = TM_PROJ
    tps = seq // tm
    row = lambda i: (i, 0)
    bf16 = jnp.bfloat16
    of_layer = lambda a: pl.BlockSpec((None,) + a.shape[1:], lambda i: (layer, 0, 0))
    return pl.pallas_call(
        functools.partial(_proj_kernel, tiles_per_seq=tps),
        grid=(t // tm,),
        in_specs=[
            pl.BlockSpec((tm, d), row),
            pl.BlockSpec((1, N_MOD, d), lambda i: (i // tps, 0, 0)),
            of_layer(g_pre), of_layer(wqkv), of_layer(wf), of_layer(bfw), of_layer(wcv),
            pl.BlockSpec(tri.shape, lambda i: (0, 0)),
        ],
        out_specs=[
            pl.BlockSpec((tm, ATTN_WIDTH), row),
            pl.BlockSpec((tm, ATTN_WIDTH), row),
            pl.BlockSpec((tm, ATTN_WIDTH), row),
            pl.BlockSpec((tm, LANES), row),
            pl.BlockSpec((tm, LANES), row),
            pl.BlockSpec((tm, CONV_WIDTH), row),
        ],
        out_shape=[
            jax.ShapeDtypeStruct((t, ATTN_WIDTH), bf16),
            jax.ShapeDtypeStruct((t, ATTN_WIDTH), bf16),
            jax.ShapeDtypeStruct((t, ATTN_WIDTH), bf16),
            jax.ShapeDtypeStruct((t, LANES), bf16),
            jax.ShapeDtypeStruct((t, LANES), bf16),
            jax.ShapeDtypeStruct((t, CONV_WIDTH), jnp.float32),
        ],
        scratch_shapes=[pltpu.VMEM((1, LANES), jnp.float32)],
        compiler_params=pltpu.CompilerParams(
            dimension_semantics=("arbitrary",), vmem_limit_bytes=VMEM_LIMIT),
        name="in_proj",
    )(x2, mod, g_pre, wqkv, wf, bfw, wcv, tri)


def _conv_chunk(rot_ref, cw_ref, cb_ref, cv_ref, i0, built):
    n = CONV_TILES
    off0 = SUBLANES * HALO_TILES - (CONV_K - 1)
    need = i0 + n + max((j + off0) // SUBLANES for j in range(CONV_K) if (j + off0) % SUBLANES)
    if need > built:
        nb = need - built
        sub = lax.broadcasted_iota(jnp.int32, (nb, SUBLANES, LANES), 1)
        w = rot_ref[0, built:built + nb + 1]
        for rho in range(1, SUBLANES):
            wr = pltpu.roll(w, SUBLANES - rho, axis=1)
            rot_ref[rho, built:built + nb] = jnp.where(sub < SUBLANES - rho, wr[0:nb], wr[1:nb + 1])
        built = need
    acc = jnp.broadcast_to(cb_ref[...], (n, SUBLANES, LANES))
    for j in range(CONV_K):
        rho, m = (j + off0) % SUBLANES, (j + off0) // SUBLANES
        acc = acc + cw_ref[j] * rot_ref[rho, i0 + m:i0 + m + n]
    cv_ref[0, i0:i0 + n] = acc
    return built


def _attn_kernel(q_ref, k_ref, v_ref, qa_ref, ka_ref, u_ref, cw_ref, cb_ref, o_ref, cv_ref,
                 kk_ref, vv_ref, s_ref, p_ref, rot_ref):
    pair = pl.program_id(1)
    seq = q_ref.shape[1]
    tq = TQ
    bf16 = jnp.bfloat16
    nt = (((1,), (1,)), ((), ()))

    kk_ref[:, 0:LANES] = k_ref[0]
    kk_ref[:, LANES:2 * LANES] = ka_ref[0]
    vv_ref[:, 0:LANES] = v_ref[0]
    vv_ref[:, LANES:2 * LANES] = jnp.ones((seq, LANES), bf16)
    rot_ref[0, 0:HALO_TILES] = jnp.zeros((HALO_TILES, SUBLANES, LANES), jnp.float32)
    rot_ref[0, HALO_TILES:] = u_ref[0]

    lane = lax.broadcasted_iota(jnp.int32, (tq, LANES), 1)
    head_of_lane = lane // AUG_STRIDE
    row = lax.broadcasted_iota(jnp.int32, (2 * tq, tq), 0) % tq
    col = lax.broadcasted_iota(jnp.int32, (2 * tq, tq), 1)
    causal = col <= row

    n_conv = seq // SUBLANES // CONV_TILES
    conv_done = 0
    built = 0
    n_chunks_total = sum(range(1, seq // tq + 1))
    chunk_no = 0

    for qi in range(seq // tq):
        q = q_ref[0, qi * tq:(qi + 1) * tq, :].astype(jnp.float32)
        qa = qa_ref[0, qi * tq:(qi + 1) * tq, :].astype(jnp.float32)
        zq = jnp.zeros_like(q)
        lhs = jnp.concatenate([
            jnp.concatenate([jnp.where(lane < HEAD_DIM, q, zq),
                             jnp.where(head_of_lane == 2 * pair, qa, zq)], axis=1),
            jnp.concatenate([jnp.where(lane >= HEAD_DIM, q, zq),
                             jnp.where(head_of_lane == 2 * pair + 1, qa, zq)], axis=1),
        ], axis=0).astype(bf16)

        nk = (qi + 1) * tq
        mrun = jnp.full((2 * tq, LANES), NEG_BIG, jnp.float32)
        for c0 in range(0, nk, tq):
            s = lax.dot_general(lhs, kk_ref[c0:c0 + tq, :], nt, preferred_element_type=jnp.float32)
            if c0 + tq == nk:
                s = jnp.where(causal, s, NEG_BIG)
            s_ref[:, c0:c0 + tq] = s
            for l0 in range(0, tq, LANES):
                mrun = jnp.maximum(mrun, s[:, l0:l0 + LANES])
            chunk_no += 1
            while conv_done * n_chunks_total < chunk_no * n_conv:
                built = _conv_chunk(rot_ref, cw_ref, cb_ref, cv_ref, conv_done * CONV_TILES, built)
                conv_done += 1
        m = jnp.max(mrun, axis=1, keepdims=True)
        for c0 in range(0, nk, tq):
            p_ref[:, c0:c0 + tq] = jnp.exp2(s_ref[:, c0:c0 + tq] - m).astype(bf16)
        acc = jnp.dot(p_ref[:, 0:nk], vv_ref[0:nk, :], preferred_element_type=jnp.float32)
        denom = acc[:, LANES:LANES + 1]
        o0 = acc[0:tq, 0:LANES] / denom[0:tq]
        o1 = acc[tq:2 * tq, 0:LANES] / denom[tq:2 * tq]
        o_ref[0, qi * tq:(qi + 1) * tq, :] = jnp.where(lane < HEAD_DIM, o0, o1).astype(o_ref.dtype)


def _attn_call(q, k, v, qa, ka, u4, cwb, cbb, *, layer):
    b, s, w = q.shape
    n_pairs = w // LANES
    tiles = s // SUBLANES
    head_blk = lambda bi, p: (bi, 0, p)
    shared_blk = lambda bi, p: (bi, 0, 0)
    return pl.pallas_call(
        _attn_kernel,
        grid=(b, n_pairs),
        in_specs=[
            pl.BlockSpec((1, s, LANES), head_blk),
            pl.BlockSpec((1, s, LANES), head_blk),
            pl.BlockSpec((1, s, LANES), head_blk),
            pl.BlockSpec((1, s, LANES), shared_blk),
            pl.BlockSpec((1, s, LANES), shared_blk),
            pl.BlockSpec((1, tiles, SUBLANES, LANES), lambda bi, p: (bi, 0, 0, p)),
            pl.BlockSpec((None, CONV_K, SUBLANES, LANES), lambda bi, p: (layer, 0, 0, p)),
            pl.BlockSpec((None, SUBLANES, LANES), lambda bi, p: (layer, 0, p)),
        ],
        out_specs=[
            pl.BlockSpec((1, s, LANES), head_blk),
            pl.BlockSpec((1, tiles, SUBLANES, LANES), lambda bi, p: (bi, 0, 0, p)),
        ],
        out_shape=[
            jax.ShapeDtypeStruct((b, s, w), jnp.bfloat16),
            jax.ShapeDtypeStruct(u4.shape, jnp.float32),
        ],
        scratch_shapes=[pltpu.VMEM((s, 2 * LANES), jnp.bfloat16),
                        pltpu.VMEM((s, 2 * LANES), jnp.bfloat16),
                        pltpu.VMEM((2 * TQ, s), jnp.float32),
                        pltpu.VMEM((2 * TQ, s), jnp.bfloat16),
                        pltpu.VMEM((SUBLANES, HALO_TILES + tiles, SUBLANES, LANES), jnp.float32)],
        compiler_params=pltpu.CompilerParams(
            dimension_semantics=("arbitrary", "arbitrary"), vmem_limit_bytes=VMEM_LIMIT),
        name="fox_attention_conv",
    )(q, k, v, qa, ka, u4, cwb, cbb)


def _mixffn_kernel(cv_ref, attn_ref, x_ref, mod_ref, lg_ref, lb_ref, wo_ref,
                   gpost_ref, gfpre_ref, win_ref, wout_ref, gfpost_ref, xo_ref, a_ref):
    bf16 = jnp.bfloat16
    g1 = mod_ref[0, 2:3, :]
    sh2 = mod_ref[0, 3:4, :]
    sc2 = mod_ref[0, 4:5, :]
    g2 = mod_ref[0, 5:6, :]
    cv = cv_ref[...]
    mu = jnp.mean(cv, axis=-1, keepdims=True)
    cen = cv - mu
    var = jnp.mean(cen * cen, axis=-1, keepdims=True)
    yl = cen * lax.rsqrt(var + EPS) * lg_ref[...] + lb_ref[...]
    act = (yl * jax.nn.sigmoid(yl)).astype(bf16)

    y = (jnp.dot(attn_ref[...], wo_ref[0:ATTN_WIDTH, :], preferred_element_type=jnp.float32)
         + jnp.dot(act, wo_ref[ATTN_WIDTH:, :], preferred_element_type=jnp.float32))
    xn = x_ref[...] + g1 * _rms(y, gpost_ref[...])
    h = (_rms(xn, gfpre_ref[...]) * (1.0 + sc2) + sh2).astype(bf16)

    for c0 in range(0, D_FF, FF_CHUNK):
        g = jnp.dot(h, win_ref[:, c0:c0 + FF_CHUNK], preferred_element_type=jnp.float32)
        u = jnp.dot(h, win_ref[:, D_FF + c0:D_FF + c0 + FF_CHUNK], preferred_element_type=jnp.float32)
        a_ref[:, c0:c0 + FF_CHUNK] = (g * jax.nn.sigmoid(g) * u).astype(bf16)
    y2 = jnp.dot(a_ref[...], wout_ref[...], preferred_element_type=jnp.float32)
    xo_ref[...] = xn + g2 * _rms(y2, gfpost_ref[...])


def _mixffn_call(cv, attn, x2, mod, lg, lb, w_o, gpost, gfpre, w_in, w_out, gfpost, *, layer, seq):
    t, d = x2.shape
    tm = TM_FFN
    tps = seq // tm
    row = lambda i: (i, 0)
    once = pl.Buffered(1)
    resident = lambda a: pl.BlockSpec((None,) + a.shape[1:], lambda i: (layer, 0, 0), pipeline_mode=once)
    return pl.pallas_call(
        _mixffn_kernel,
        grid=(t // tm,),
        in_specs=[
            pl.BlockSpec((tm, CONV_WIDTH), row),
            pl.BlockSpec((tm, ATTN_WIDTH), row),
            pl.BlockSpec((tm, d), row),
            pl.BlockSpec((1, N_MOD, d), lambda i: (i // tps, 0, 0)),
            resident(lg), resident(lb), resident(w_o),
            resident(gpost), resident(gfpre), resident(w_in), resident(w_out), resident(gfpost),
        ],
        out_specs=pl.BlockSpec((tm, d), row),
        out_shape=jax.ShapeDtypeStruct((t, d), jnp.float32),
        scratch_shapes=[pltpu.VMEM((tm, D_FF), jnp.bfloat16)],
        compiler_params=pltpu.CompilerParams(
            dimension_semantics=("arbitrary",), vmem_limit_bytes=VMEM_LIMIT),
        name="outproj_ffn",
    )(cv, attn, x2, mod, lg, lb, w_o, gpost, gfpre, w_in, w_out, gfpost)


def _forget_weights(w_f, b_f):
    depth, d, _ = w_f.shape
    keep = (jnp.arange(AUG_STRIDE) < 6).astype(w_f.dtype)
    wf = (jnp.repeat(w_f[..., None], AUG_STRIDE, axis=-1) * keep).reshape(depth, d, N_HEADS * AUG_STRIDE)
    bf = (jnp.repeat(b_f[..., None], AUG_STRIDE, axis=-1) * keep).reshape(depth, 1, N_HEADS * AUG_STRIDE)
    return wf, bf


def kernel(x, c, w_in, b_f, conv_w, conv_b, conv_ln_g, conv_ln_b, w_o, w_ffn_in, w_ffn_out,
           mix_pre_g, mix_post_g, ffn_pre_g, ffn_post_g, ada_w, ada_b):
    b, s, d = x.shape
    depth = w_in.shape[0]
    bf16 = jnp.bfloat16
    aw, cw = ATTN_WIDTH, CONV_WIDTH

    mod_all = _ada_call(c, ada_w, ada_b).reshape(depth, b, N_MOD, d)
    tri = jnp.tri(TM_PROJ, dtype=bf16)
    x2 = x.reshape(b * s, d)

    wqkv = w_in[:, :, 0:3 * aw].astype(bf16)
    wf, bfw = _forget_weights(w_in[:, :, 3 * aw:3 * aw + N_HEADS], b_f)
    wf = wf.astype(bf16)
    wcv = w_in[:, :, 3 * aw + N_HEADS:].astype(bf16)
    w_o_b = w_o.astype(bf16)
    w_fi_b = w_ffn_in.astype(bf16)
    w_fo_b = w_ffn_out.astype(bf16)
    cwb = jnp.broadcast_to(conv_w[:, :, None, :], (depth, CONV_K, SUBLANES, cw))
    cbb = jnp.broadcast_to(conv_b[:, None, :], (depth, SUBLANES, cw))
    vec = lambda a: a.reshape(depth, 1, a.shape[-1])

    for l in range(depth):
        mod = mod_all[l]
        q, k, v, qa, ka, u = _proj_call(
            x2, mod, vec(mix_pre_g), wqkv, wf, bfw, wcv, tri, layer=l, seq=s)
        attn, cv = _attn_call(
            q.reshape(b, s, aw), k.reshape(b, s, aw), v.reshape(b, s, aw),
            qa.reshape(b, s, LANES), ka.reshape(b, s, LANES),
            u.reshape(b, s // SUBLANES, SUBLANES, cw), cwb, cbb, layer=l)
        x2 = _mixffn_call(
            cv.reshape(b * s, cw), attn.reshape(b * s, aw), x2, mod,
            vec(conv_ln_g), vec(conv_ln_b), w_o_b, vec(mix_post_g), vec(ffn_pre_g),
            w_fi_b, w_fo_b, vec(ffn_post_g), layer=l, seq=s)
    return x2.reshape(b, s, d)
```

```python
import functools
import math

import jax
import jax.numpy as jnp
from jax import lax
from jax.experimental import pallas as pl
from jax.experimental.pallas import tpu as pltpu

D_MODEL = 1024
ATTN_WIDTH = 512
HEAD_DIM = 64
N_HEADS = 8
CONV_WIDTH = 512
CONV_K = 31
D_FF = 2816
N_MOD = 6
EPS = 1e-6

LANES = 128
SUBLANES = 8
AUG_STRIDE = 16
NEG_BIG = -1e30
LOG2E = math.log2(math.e)

TM_PROJ = 512
TQ = 256
TM_FFN = 512
FF_CHUNK = 256
HALO_TILES = 4
CONV_TILES = 8
VMEM_LIMIT = 56 * 1024 * 1024


def _split3(x):
    hi = x.astype(jnp.bfloat16).astype(jnp.float32)
    r = x - hi
    mid = r.astype(jnp.bfloat16).astype(jnp.float32)
    lo = (r - mid).astype(jnp.bfloat16).astype(jnp.float32)
    return hi, mid, lo


def _rms(x, g):
    return x * lax.rsqrt(jnp.mean(x * x, axis=-1, keepdims=True) + EPS) * g


def _dep_zero(x, zero_bits):
    bits = lax.bitcast_convert_type(x, jnp.uint32) & zero_bits
    return lax.bitcast_convert_type(bits, jnp.float32)


def _ada_kernel(c_ref, w_ref, b_ref, o_ref):
    c = c_ref[...]
    ca = (c * jax.nn.sigmoid(c)).astype(jnp.bfloat16)
    w = w_ref[0].astype(jnp.bfloat16)
    o_ref[0] = jnp.dot(ca, w, preferred_element_type=jnp.float32) + b_ref[0]


def _ada_call(c, ada_w, ada_b):
    depth, d, n = ada_w.shape
    b = c.shape[0]
    tn = 1536
    return pl.pallas_call(
        _ada_kernel,
        grid=(depth, n // tn),
        in_specs=[
            pl.BlockSpec((b, d), lambda l, j: (0, 0)),
            pl.BlockSpec((1, d, tn), lambda l, j: (l, 0, j)),
            pl.BlockSpec((1, 1, tn), lambda l, j: (l, 0, j)),
        ],
        out_specs=pl.BlockSpec((1, b, tn), lambda l, j: (l, 0, j)),
        out_shape=jax.ShapeDtypeStruct((depth, b, n), jnp.float32),
        compiler_params=pltpu.CompilerParams(
            dimension_semantics=("arbitrary", "arbitrary"), vmem_limit_bytes=VMEM_LIMIT),
        name="adaln_mod",
    )(c, ada_w, ada_b.reshape(depth, 1, n))


def _proj_kernel(x_ref, mod_ref, g_ref, wqkv_ref, wf_ref, bf_ref, wcv_ref, tri_ref,
                 q_ref, k_ref, v_ref, qa_ref, ka_ref, u_ref, carry_ref, *, tiles_per_seq):
    si = pl.program_id(0) % tiles_per_seq

    @pl.when(si == 0)
    def _():
        carry_ref[...] = jnp.zeros_like(carry_ref)

    x = x_ref[...]
    sh = mod_ref[0, 0:1, :]
    sc = mod_ref[0, 1:2, :]
    h = _rms(x, g_ref[...]) * (1.0 + sc) + sh
    hb = h.astype(jnp.bfloat16)

    aw = ATTN_WIDTH
    q = jnp.dot(hb, wqkv_ref[:, 0:aw], preferred_element_type=jnp.float32)
    q_ref[...] = (q * (HEAD_DIM ** -0.5 * LOG2E)).astype(jnp.bfloat16)
    k_ref[...] = jnp.dot(hb, wqkv_ref[:, aw:2 * aw], preferred_element_type=jnp.float32).astype(jnp.bfloat16)
    v_ref[...] = jnp.dot(hb, wqkv_ref[:, 2 * aw:3 * aw], preferred_element_type=jnp.float32).astype(jnp.bfloat16)

    fl = jnp.dot(hb, wf_ref[...], preferred_element_type=jnp.float32) + bf_ref[...]
    logf = -(jnp.maximum(-fl, 0.0) + jnp.log1p(jnp.exp(-jnp.abs(fl))))
    hi, mid, lo = _split3(logf)
    parts = jnp.concatenate([hi, mid, lo], axis=1).astype(jnp.bfloat16)
    cp = jnp.dot(tri_ref[...], parts, preferred_element_type=jnp.float32)
    cum = cp[:, 0:LANES] + cp[:, LANES:2 * LANES] + cp[:, 2 * LANES:3 * LANES] + carry_ref[...]
    tm = cum.shape[0]
    carry_ref[...] = cum[tm - 1:tm, :]

    chi, cmid, clo = _split3(cum * LOG2E)
    one = jnp.ones_like(chi)
    zero = jnp.zeros_like(chi)
    lane = lax.broadcasted_iota(jnp.int32, cum.shape, 1) % AUG_STRIDE
    qa_ref[...] = jnp.where(lane == 0, chi, jnp.where(lane == 1, cmid, jnp.where(
        lane == 2, clo, jnp.where(lane < 6, one, zero)))).astype(qa_ref.dtype)
    ka_ref[...] = jnp.where(lane < 3, one, jnp.where(lane == 3, -chi, jnp.where(
        lane == 4, -cmid, jnp.where(lane == 5, -clo, zero)))).astype(ka_ref.dtype)

    cw = CONV_WIDTH
    val = jnp.dot(hb, wcv_ref[:, 0:cw], preferred_element_type=jnp.float32)
    gate = jnp.dot(hb, wcv_ref[:, cw:2 * cw], preferred_element_type=jnp.float32)
    u_ref[...] = val * jax.nn.sigmoid(gate)


def _proj_call(x2, mod, g_pre, wqkv, wf, bfw, wcv, tri, *, layer, seq):
    t, d = x2.shape
    tm = TM_PROJ
    tps = seq // tm
    row = lambda i: (i, 0)
    bf16 = jnp.bfloat16
    of_layer = lambda a: pl.BlockSpec((None,) + a.shape[1:], lambda i: (layer, 0, 0))
    return pl.pallas_call(
        functools.partial(_proj_kernel, tiles_per_seq=tps),
        grid=(t // tm,),
        in_specs=[
            pl.BlockSpec((tm, d), row),
            pl.BlockSpec((1, N_MOD, d), lambda i: (i // tps, 0, 0)),
            of_layer(g_pre), of_layer(wqkv), of_layer(wf), of_layer(bfw), of_layer(wcv),
            pl.BlockSpec(tri.shape, lambda i: (0, 0)),
        ],
        out_specs=[
            pl.BlockSpec((tm, ATTN_WIDTH), row),
            pl.BlockSpec((tm, ATTN_WIDTH), row),
            pl.BlockSpec((tm, ATTN_WIDTH), row),
            pl.BlockSpec((tm, LANES), row),
            pl.BlockSpec((tm, LANES), row),
            pl.BlockSpec((tm, CONV_WIDTH), row),
        ],
        out_shape=[
            jax.ShapeDtypeStruct((t, ATTN_WIDTH), bf16),
            jax.ShapeDtypeStruct((t, ATTN_WIDTH), bf16),
            jax.ShapeDtypeStruct((t, ATTN_WIDTH), bf16),
            jax.ShapeDtypeStruct((t, LANES), bf16),
            jax.ShapeDtypeStruct((t, LANES), bf16),
            jax.ShapeDtypeStruct((t, CONV_WIDTH), jnp.float32),
        ],
        scratch_shapes=[pltpu.VMEM((1, LANES), jnp.float32)],
        compiler_params=pltpu.CompilerParams(
            dimension_semantics=("arbitrary",), vmem_limit_bytes=VMEM_LIMIT),
        name="in_proj",
    )(x2, mod, g_pre, wqkv, wf, bfw, wcv, tri)


def _attn_kernel(q_ref, k_ref, v_ref, qa_ref, ka_ref, o_ref, kk_ref, vv_ref, s_ref, p_ref):
    pair = pl.program_id(1)
    seq = q_ref.shape[1]
    tq = TQ
    bf16 = jnp.bfloat16
    nt = (((1,), (1,)), ((), ()))

    kk_ref[:, 0:LANES] = k_ref[0]
    kk_ref[:, LANES:2 * LANES] = ka_ref[0]
    vv_ref[:, 0:LANES] = v_ref[0]
    vv_ref[:, LANES:2 * LANES] = jnp.ones((seq, LANES), bf16)

    lane = lax.broadcasted_iota(jnp.int32, (tq, LANES), 1)
    head_of_lane = lane // AUG_STRIDE
    row = lax.broadcasted_iota(jnp.int32, (2 * tq, tq), 0) % tq
    col = lax.broadcasted_iota(jnp.int32, (2 * tq, tq), 1)
    causal = col <= row

    for qi in range(seq // tq):
        q = q_ref[0, qi * tq:(qi + 1) * tq, :].astype(jnp.float32)
        qa = qa_ref[0, qi * tq:(qi + 1) * tq, :].astype(jnp.float32)
        zq = jnp.zeros_like(q)
        lhs = jnp.concatenate([
            jnp.concatenate([jnp.where(lane < HEAD_DIM, q, zq),
                             jnp.where(head_of_lane == 2 * pair, qa, zq)], axis=1),
            jnp.concatenate([jnp.where(lane >= HEAD_DIM, q, zq),
                             jnp.where(head_of_lane == 2 * pair + 1, qa, zq)], axis=1),
        ], axis=0).astype(bf16)

        nk = (qi + 1) * tq
        mrun = jnp.full((2 * tq, LANES), NEG_BIG, jnp.float32)
        for c0 in range(0, nk, tq):
            s = lax.dot_general(lhs, kk_ref[c0:c0 + tq, :], nt, preferred_element_type=jnp.float32)
            if c0 + tq == nk:
                s = jnp.where(causal, s, NEG_BIG)
            s_ref[:, c0:c0 + tq] = s
            for l0 in range(0, tq, LANES):
                mrun = jnp.maximum(mrun, s[:, l0:l0 + LANES])
        m = jnp.max(mrun, axis=1, keepdims=True)
        for c0 in range(0, nk, tq):
            p_ref[:, c0:c0 + tq] = jnp.exp2(s_ref[:, c0:c0 + tq] - m).astype(bf16)
        acc = jnp.dot(p_ref[:, 0:nk], vv_ref[0:nk, :], preferred_element_type=jnp.float32)
        denom = acc[:, LANES:LANES + 1]
        o0 = acc[0:tq, 0:LANES] / denom[0:tq]
        o1 = acc[tq:2 * tq, 0:LANES] / denom[tq:2 * tq]
        o_ref[0, qi * tq:(qi + 1) * tq, :] = jnp.where(lane < HEAD_DIM, o0, o1).astype(o_ref.dtype)


def _attn_call(q, k, v, qa, ka):
    b, s, w = q.shape
    n_pairs = w // LANES
    head_blk = lambda bi, p: (bi, 0, p)
    shared_blk = lambda bi, p: (bi, 0, 0)
    return pl.pallas_call(
        _attn_kernel,
        grid=(b, n_pairs),
        in_specs=[
            pl.BlockSpec((1, s, LANES), head_blk),
            pl.BlockSpec((1, s, LANES), head_blk),
            pl.BlockSpec((1, s, LANES), head_blk),
            pl.BlockSpec((1, s, LANES), shared_blk),
            pl.BlockSpec((1, s, LANES), shared_blk),
        ],
        out_specs=pl.BlockSpec((1, s, LANES), head_blk),
        out_shape=jax.ShapeDtypeStruct((b, s, w), jnp.bfloat16),
        scratch_shapes=[pltpu.VMEM((s, 2 * LANES), jnp.bfloat16),
                        pltpu.VMEM((s, 2 * LANES), jnp.bfloat16),
                        pltpu.VMEM((2 * TQ, s), jnp.float32),
                        pltpu.VMEM((2 * TQ, s), jnp.bfloat16)],
        compiler_params=pltpu.CompilerParams(
            dimension_semantics=("arbitrary", "arbitrary"), vmem_limit_bytes=VMEM_LIMIT),
        name="fox_attention",
    )(q, k, v, qa, ka)


def _conv_chunk(rot_ref, cw_ref, cb_ref, out_ref, lanes, i0, built):
    n = CONV_TILES
    off0 = SUBLANES * HALO_TILES - (CONV_K - 1)
    need = i0 + n + max((j + off0) // SUBLANES for j in range(CONV_K) if (j + off0) % SUBLANES)
    if need > built:
        nb = need - built
        sub = lax.broadcasted_iota(jnp.int32, (nb, SUBLANES, LANES), 1)
        w = rot_ref[0, built:built + nb + 1]
        for rho in range(1, SUBLANES):
            wr = pltpu.roll(w, SUBLANES - rho, axis=1)
            rot_ref[rho, built:built + nb] = jnp.where(sub < SUBLANES - rho, wr[0:nb], wr[1:nb + 1])
        built = need
    acc = jnp.broadcast_to(cb_ref[:, lanes], (n, SUBLANES, LANES))
    for j in range(CONV_K):
        rho, m = (j + off0) % SUBLANES, (j + off0) // SUBLANES
        acc = acc + cw_ref[j, :, lanes] * rot_ref[rho, i0 + m:i0 + m + n]
    out_ref[i0:i0 + n, :, lanes] = acc
    return built, acc[0]


def _conv_pieces(cur_ref, halo, rot_ref, cw_ref, cb_ref, out_ref):
    tiles = cur_ref.shape[0]
    pieces = []
    for lb in range(CONV_WIDTH // LANES):
        lanes = slice(lb * LANES, (lb + 1) * LANES)
        state = {"built": 0}
        for i0 in range(0, tiles, CONV_TILES):
            def piece(lanes=lanes, i0=i0, state=state, rot=rot_ref.at[lb % 2]):
                if i0 == 0:
                    rot[0, 0:HALO_TILES] = halo[:, :, lanes]
                    rot[0, HALO_TILES:] = cur_ref[:, :, lanes]
                state["built"], tile0 = _conv_chunk(rot, cw_ref, cb_ref, out_ref, lanes, i0, state["built"])
                return tile0
            pieces.append(piece)
    return pieces


def _mixffn_kernel(zero_ref, u0_ref, unext_ref, halo_ref, attn_ref, x_ref, mod_ref, cw_ref, cb_ref, lg_ref, lb_ref,
                   wo_ref, gpost_ref, gfpre_ref, win_ref, wout_ref, gfpost_ref, xo_ref,
                   a_ref, cv_ref, cvn_ref, rot_ref, h_ref, *, tiles_per_seq):
    i = pl.program_id(0)
    bf16 = jnp.bfloat16
    zero_halo = jnp.zeros(halo_ref.shape, jnp.float32)

    @pl.when(i == 0)
    def _():
        for piece in _conv_pieces(u0_ref, zero_halo, rot_ref, cw_ref, cb_ref, cv_ref):
            piece()

    @pl.when(i > 0)
    def _():
        cv_ref[...] = cvn_ref[...]

    g1 = mod_ref[0, 2:3, :]
    sh2 = mod_ref[0, 3:4, :]
    sc2 = mod_ref[0, 4:5, :]
    g2 = mod_ref[0, 5:6, :]

    cv = cv_ref[...].reshape(x_ref.shape[0], CONV_WIDTH)
    mu = jnp.mean(cv, axis=-1, keepdims=True)
    cen = cv - mu
    var = jnp.mean(cen * cen, axis=-1, keepdims=True)
    yl = cen * lax.rsqrt(var + EPS) * lg_ref[...] + lb_ref[...]
    act = (yl * jax.nn.sigmoid(yl)).astype(bf16)

    y = (jnp.dot(attn_ref[...], wo_ref[0:ATTN_WIDTH, :], preferred_element_type=jnp.float32)
         + jnp.dot(act, wo_ref[ATTN_WIDTH:, :], preferred_element_type=jnp.float32))
    xn = x_ref[...] + g1 * _rms(y, gpost_ref[...])
    hf = _rms(xn, gfpre_ref[...]) * (1.0 + sc2) + sh2
    h_ref[...] = hf.astype(bf16)
    h_corner = hf[0:2 * SUBLANES, 0:LANES]

    starts_seq = (i + 1) % tiles_per_seq == 0
    halo = jnp.where(starts_seq, zero_halo, halo_ref[...])
    pieces = _conv_pieces(unext_ref, halo, rot_ref, cw_ref, cb_ref, cvn_ref)
    n_ff = D_FF // FF_CHUNK
    n_dots = 2 * n_ff
    state = {"emitted": 0, "dots": 0, "token": None}

    def ffn_dot(col0):
        if state["token"] is not None:
            z = _dep_zero(state["token"], zero_ref[...])
            h_ref[0:2 * SUBLANES, 0:LANES] = (h_corner + jnp.concatenate([z, z], axis=0)).astype(bf16)
            state["token"] = None
        out = jnp.dot(h_ref[...], win_ref[:, col0:col0 + FF_CHUNK], preferred_element_type=jnp.float32)
        state["dots"] += 1
        target = -(-len(pieces) * state["dots"] // (n_dots - 1))
        while state["emitted"] < min(target, len(pieces)):
            tile0 = pieces[state["emitted"]]()
            state["token"] = tile0 if state["token"] is None else state["token"] + tile0
            state["emitted"] += 1
        return out

    for ci in range(n_ff):
        c0 = ci * FF_CHUNK
        g = ffn_dot(c0)
        u = ffn_dot(D_FF + c0)
        a_ref[:, c0:c0 + FF_CHUNK] = (g * jax.nn.sigmoid(g) * u).astype(bf16)
    y2 = jnp.dot(a_ref[...], wout_ref[...], preferred_element_type=jnp.float32)
    xo_ref[...] = xn + g2 * _rms(y2, gfpost_ref[...])


def _mixffn_call(u3, attn, x2, mod, cwb, cbb, lg, lb, w_o, gpost, gfpre, w_in, w_out, gfpost, *, layer, seq):
    t, d = x2.shape
    tm = TM_FFN
    tps = seq // tm
    n_tiles = t // tm
    tiles = tm // SUBLANES
    row = lambda i: (i, 0)
    nxt = lambda i: jnp.minimum(i + 1, n_tiles - 1)
    once = pl.Buffered(1)

    def resident(a):
        zeros = (0,) * (a.ndim - 1)
        return pl.BlockSpec((None,) + a.shape[1:], lambda i: (layer,) + zeros, pipeline_mode=once)

    return pl.pallas_call(
        functools.partial(_mixffn_kernel, tiles_per_seq=tps),
        grid=(n_tiles,),
        in_specs=[
            pl.BlockSpec((SUBLANES, LANES), lambda i: (0, 0)),
            pl.BlockSpec((tiles, SUBLANES, CONV_WIDTH), lambda i: (0, 0, 0), pipeline_mode=once),
            pl.BlockSpec((tiles, SUBLANES, CONV_WIDTH), lambda i: (nxt(i), 0, 0)),
            pl.BlockSpec((HALO_TILES, SUBLANES, CONV_WIDTH),
                         lambda i: (nxt(i) * (tiles // HALO_TILES) - 1, 0, 0)),
            pl.BlockSpec((tm, ATTN_WIDTH), row),
            pl.BlockSpec((tm, d), row),
            pl.BlockSpec((1, N_MOD, d), lambda i: (i // tps, 0, 0)),
            resident(cwb), resident(cbb), resident(lg), resident(lb), resident(w_o),
            resident(gpost), resident(gfpre), resident(w_in), resident(w_out), resident(gfpost),
        ],
        out_specs=pl.BlockSpec((tm, d), row),
        out_shape=jax.ShapeDtypeStruct((t, d), jnp.float32),
        scratch_shapes=[pltpu.VMEM((tm, D_FF), jnp.bfloat16),
                        pltpu.VMEM((tiles, SUBLANES, CONV_WIDTH), jnp.float32),
                        pltpu.VMEM((tiles, SUBLANES, CONV_WIDTH), jnp.float32),
                        pltpu.VMEM((2, SUBLANES, HALO_TILES + tiles, SUBLANES, LANES), jnp.float32),
                        pltpu.VMEM((tm, d), jnp.bfloat16)],
        compiler_params=pltpu.CompilerParams(
            dimension_semantics=("arbitrary",), vmem_limit_bytes=VMEM_LIMIT),
        name="conv_outproj_ffn",
    )(jnp.zeros((SUBLANES, LANES), jnp.uint32), u3, u3, u3, attn, x2, mod,
      cwb, cbb, lg, lb, w_o, gpost, gfpre, w_in, w_out, gfpost)


def _forget_weights(w_f, b_f):
    depth, d, _ = w_f.shape
    keep = (jnp.arange(AUG_STRIDE) < 6).astype(w_f.dtype)
    wf = (jnp.repeat(w_f[..., None], AUG_STRIDE, axis=-1) * keep).reshape(depth, d, N_HEADS * AUG_STRIDE)
    bf = (jnp.repeat(b_f[..., None], AUG_STRIDE, axis=-1) * keep).reshape(depth, 1, N_HEADS * AUG_STRIDE)
    return wf, bf


def kernel(x, c, w_in, b_f, conv_w, conv_b, conv_ln_g, conv_ln_b, w_o, w_ffn_in, w_ffn_out,
           mix_pre_g, mix_post_g, ffn_pre_g, ffn_post_g, ada_w, ada_b):
    b, s, d = x.shape
    depth = w_in.shape[0]
    bf16 = jnp.bfloat16
    aw, cw = ATTN_WIDTH, CONV_WIDTH

    mod_all = _ada_call(c, ada_w, ada_b).reshape(depth, b, N_MOD, d)
    tri = jnp.tri(TM_PROJ, dtype=bf16)
    x2 = x.reshape(b * s, d)

    wqkv = w_in[:, :, 0:3 * aw].astype(bf16)
    wf, bfw = _forget_weights(w_in[:, :, 3 * aw:3 * aw + N_HEADS], b_f)
    wf = wf.astype(bf16)
    wcv = w_in[:, :, 3 * aw + N_HEADS:].astype(bf16)
    w_o_b = w_o.astype(bf16)
    w_fi_b = w_ffn_in.astype(bf16)
    w_fo_b = w_ffn_out.astype(bf16)
    cwb = jnp.broadcast_to(conv_w[:, :, None, :], (depth, CONV_K, SUBLANES, cw))
    cbb = jnp.broadcast_to(conv_b[:, None, :], (depth, SUBLANES, cw))
    vec = lambda a: a.reshape(depth, 1, a.shape[-1])

    for l in range(depth):
        mod = mod_all[l]
        q, k, v, qa, ka, u = _proj_call(
            x2, mod, vec(mix_pre_g), wqkv, wf, bfw, wcv, tri, layer=l, seq=s)
        attn = _attn_call(
            q.reshape(b, s, aw), k.reshape(b, s, aw), v.reshape(b, s, aw),
            qa.reshape(b, s, LANES), ka.reshape(b, s, LANES))
        x2 = _mixffn_call(
            u.reshape(b * s // SUBLANES, SUBLANES, cw), attn.reshape(b * s, aw), x2, mod,
            cwb, cbb, vec(conv_ln_g), vec(conv_ln_b), w_o_b, vec(mix_post_g), vec(ffn_pre_g),
            w_fi_b, w_fo_b, vec(ffn_post_g), layer=l, seq=s)
    return x2.reshape(b, s, d)
```

```python
import functools
import math

import jax
import jax.numpy as jnp
from jax import lax
from jax.experimental import pallas as pl
from jax.experimental.pallas import tpu as pltpu

D_MODEL = 1024
ATTN_WIDTH = 512
HEAD_DIM = 64
N_HEADS = 8
CONV_WIDTH = 512
CONV_K = 31
D_FF = 2816
N_MOD = 6
EPS = 1e-6

LANES = 128
SUBLANES = 8
AUG_STRIDE = 16
NEG_BIG = -1e30
LOG2E = math.log2(math.e)

TM_PROJ = 512
TQ = 256
TM_FFN = 512
FF_CHUNK = 256
HALO_TILES = 4
CONV_TILES = 8
VMEM_LIMIT = 56 * 1024 * 1024


def _split3(x):
    hi = x.astype(jnp.bfloat16).astype(jnp.float32)
    r = x - hi
    mid = r.astype(jnp.bfloat16).astype(jnp.float32)
    lo = (r - mid).astype(jnp.bfloat16).astype(jnp.float32)
    return hi, mid, lo


def _rms(x, g):
    return x * lax.rsqrt(jnp.mean(x * x, axis=-1, keepdims=True) + EPS) * g


def _ada_kernel(c_ref, w_ref, b_ref, o_ref):
    c = c_ref[...]
    ca = (c * jax.nn.sigmoid(c)).astype(jnp.bfloat16)
    w = w_ref[0].astype(jnp.bfloat16)
    o_ref[0] = jnp.dot(ca, w, preferred_element_type=jnp.float32) + b_ref[0]


def _ada_call(c, ada_w, ada_b):
    depth, d, n = ada_w.shape
    b = c.shape[0]
    tn = 1536
    return pl.pallas_call(
        _ada_kernel,
        grid=(depth, n // tn),
        in_specs=[
            pl.BlockSpec((b, d), lambda l, j: (0, 0)),
            pl.BlockSpec((1, d, tn), lambda l, j: (l, 0, j)),
            pl.BlockSpec((1, 1, tn), lambda l, j: (l, 0, j)),
        ],
        out_specs=pl.BlockSpec((1, b, tn), lambda l, j: (l, 0, j)),
        out_shape=jax.ShapeDtypeStruct((depth, b, n), jnp.float32),
        compiler_params=pltpu.CompilerParams(
            dimension_semantics=("arbitrary", "arbitrary"), vmem_limit_bytes=VMEM_LIMIT),
        name="adaln_mod",
    )(c, ada_w, ada_b.reshape(depth, 1, n))


def _proj_kernel(x_ref, mod_ref, g_ref, wqkv_ref, wf_ref, bf_ref, wcv_ref, tri_ref,
                 q_ref, k_ref, v_ref, qa_ref, ka_ref, u_ref, carry_ref, *, tiles_per_seq):
    si = pl.program_id(0) % tiles_per_seq

    @pl.when(si == 0)
    def _():
        carry_ref[...] = jnp.zeros_like(carry_ref)

    x = x_ref[...]
    sh = mod_ref[0, 0:1, :]
    sc = mod_ref[0, 1:2, :]
    h = _rms(x, g_ref[...]) * (1.0 + sc) + sh
    hb = h.astype(jnp.bfloat16)

    aw = ATTN_WIDTH
    q = jnp.dot(hb, wqkv_ref[:, 0:aw], preferred_element_type=jnp.float32)
    q_ref[...] = (q * (HEAD_DIM ** -0.5 * LOG2E)).astype(jnp.bfloat16)
    k_ref[...] = jnp.dot(hb, wqkv_ref[:, aw:2 * aw], preferred_element_type=jnp.float32).astype(jnp.bfloat16)
    v_ref[...] = jnp.dot(hb, wqkv_ref[:, 2 * aw:3 * aw], preferred_element_type=jnp.float32).astype(jnp.bfloat16)

    fl = jnp.dot(hb, wf_ref[...], preferred_element_type=jnp.float32) + bf_ref[...]
    logf = -(jnp.maximum(-fl, 0.0) + jnp.log1p(jnp.exp(-jnp.abs(fl))))
    hi, mid, lo = _split3(logf)
    parts = jnp.concatenate([hi, mid, lo], axis=1).astype(jnp.bfloat16)
    cp = jnp.dot(tri_ref[...], parts, preferred_element_type=jnp.float32)
    cum = cp[:, 0:LANES] + cp[:, LANES:2 * LANES] + cp[:, 2 * LANES:3 * LANES] + carry_ref[...]
    tm = cum.shape[0]
    carry_ref[...] = cum[tm - 1:tm, :]

    chi, cmid, clo = _split3(cum * LOG2E)
    one = jnp.ones_like(chi)
    zero = jnp.zeros_like(chi)
    lane = lax.broadcasted_iota(jnp.int32, cum.shape, 1) % AUG_STRIDE
    qa_ref[...] = jnp.where(lane == 0, chi, jnp.where(lane == 1, cmid, jnp.where(
        lane == 2, clo, jnp.where(lane < 6, one, zero)))).astype(qa_ref.dtype)
    ka_ref[...] = jnp.where(lane < 3, one, jnp.where(lane == 3, -chi, jnp.where(
        lane == 4, -cmid, jnp.where(lane == 5, -clo, zero)))).astype(ka_ref.dtype)

    cw = CONV_WIDTH
    val = jnp.dot(hb, wcv_ref[:, 0:cw], preferred_element_type=jnp.float32)
    gate = jnp.dot(hb, wcv_ref[:, cw:2 * cw], preferred_element_type=jnp.float32)
    u_ref[...] = val * jax.nn.sigmoid(gate)


def _proj_call(x2, mod, g_pre, wqkv, wf, bfw, wcv, tri, *, layer, seq):
    t, d = x2.shape
    tm = TM_PROJ
    tps = seq // tm
    row = lambda i: (i, 0)
    bf16 = jnp.bfloat16
    of_layer = lambda a: pl.BlockSpec((None,) + a.shape[1:], lambda i: (layer, 0, 0))
    return pl.pallas_call(
        functools.partial(_proj_kernel, tiles_per_seq=tps),
        grid=(t // tm,),
        in_specs=[
            pl.BlockSpec((tm, d), row),
            pl.BlockSpec((1, N_MOD, d), lambda i: (i // tps, 0, 0)),
            of_layer(g_pre), of_layer(wqkv), of_layer(wf), of_layer(bfw), of_layer(wcv),
            pl.BlockSpec(tri.shape, lambda i: (0, 0)),
        ],
        out_specs=[
            pl.BlockSpec((tm, ATTN_WIDTH), row),
            pl.BlockSpec((tm, ATTN_WIDTH), row),
            pl.BlockSpec((tm, ATTN_WIDTH), row),
            pl.BlockSpec((tm, LANES), row),
            pl.BlockSpec((tm, LANES), row),
            pl.BlockSpec((tm, CONV_WIDTH), row),
        ],
        out_shape=[
            jax.ShapeDtypeStruct((t, ATTN_WIDTH), bf16),
            jax.ShapeDtypeStruct((t, ATTN_WIDTH), bf16),
            jax.ShapeDtypeStruct((t, ATTN_WIDTH), bf16),
            jax.ShapeDtypeStruct((t, LANES), bf16),
            jax.ShapeDtypeStruct((t, LANES), bf16),
            jax.ShapeDtypeStruct((t, CONV_WIDTH), jnp.float32),
        ],
        scratch_shapes=[pltpu.VMEM((1, LANES), jnp.float32)],
        compiler_params=pltpu.CompilerParams(
            dimension_semantics=("arbitrary",), vmem_limit_bytes=VMEM_LIMIT),
        name="in_proj",
    )(x2, mod, g_pre, wqkv, wf, bfw, wcv, tri)


def _conv_chunk(rot_ref, cw_ref, cb_ref, cv_ref, i0, built):
    n = CONV_TILES
    off0 = SUBLANES * HALO_TILES - (CONV_K - 1)
    need = i0 + n + max((j + off0) // SUBLANES for j in range(CONV_K) if (j + off0) % SUBLANES)
    if need > built:
        nb = need - built
        sub = lax.broadcasted_iota(jnp.int32, (nb, SUBLANES, LANES), 1)
        w = rot_ref[0, built:built + nb + 1]
        for rho in range(1, SUBLANES):
            wr = pltpu.roll(w, SUBLANES - rho, axis=1)
            rot_ref[rho, built:built + nb] = jnp.where(sub < SUBLANES - rho, wr[0:nb], wr[1:nb + 1])
        built = need
    acc = jnp.broadcast_to(cb_ref[...], (n, SUBLANES, LANES))
    for j in range(CONV_K):
        rho, m = (j + off0) % SUBLANES, (j + off0) // SUBLANES
        acc = acc + cw_ref[j] * rot_ref[rho, i0 + m:i0 + m + n]
    cv_ref[0, i0:i0 + n] = acc
    return built


def _attn_kernel(q_ref, k_ref, v_ref, qa_ref, ka_ref, u_ref, cw_ref, cb_ref, o_ref, cv_ref,
                 kk_ref, vv_ref, s_ref, p_ref, rot_ref):
    pair = pl.program_id(1)
    seq = q_ref.shape[1]
    tq = TQ
    n_q = seq // tq
    bf16 = jnp.bfloat16
    nt = (((1,), (1,)), ((), ()))

    kk_ref[:, 0:LANES] = k_ref[0]
    kk_ref[:, LANES:2 * LANES] = ka_ref[0]
    vv_ref[:, 0:LANES] = v_ref[0]
    vv_ref[:, LANES:2 * LANES] = jnp.ones((seq, LANES), bf16)
    rot_ref[0, 0:HALO_TILES] = jnp.zeros((HALO_TILES, SUBLANES, LANES), jnp.float32)
    rot_ref[0, HALO_TILES:] = u_ref[0]

    lane = lax.broadcasted_iota(jnp.int32, (tq, LANES), 1)
    head_of_lane = lane // AUG_STRIDE
    row = lax.broadcasted_iota(jnp.int32, (2 * tq, tq), 0) % tq
    col = lax.broadcasted_iota(jnp.int32, (2 * tq, tq), 1)
    causal = col <= row

    def query_operand(qi):
        q = q_ref[0, qi * tq:(qi + 1) * tq, :].astype(jnp.float32)
        qa = qa_ref[0, qi * tq:(qi + 1) * tq, :].astype(jnp.float32)
        zq = jnp.zeros_like(q)
        return jnp.concatenate([
            jnp.concatenate([jnp.where(lane < HEAD_DIM, q, zq),
                             jnp.where(head_of_lane == 2 * pair, qa, zq)], axis=1),
            jnp.concatenate([jnp.where(lane >= HEAD_DIM, q, zq),
                             jnp.where(head_of_lane == 2 * pair + 1, qa, zq)], axis=1),
        ], axis=0).astype(bf16)

    def logits_chunk(qi, c0, lhs, mrun):
        s = lax.dot_general(lhs, kk_ref[c0:c0 + tq, :], nt, preferred_element_type=jnp.float32)
        if c0 == qi * tq:
            s = jnp.where(causal, s, NEG_BIG)
        s_ref[qi % 2, :, c0:c0 + tq] = s
        for l0 in range(0, tq, LANES):
            mrun = jnp.maximum(mrun, s[:, l0:l0 + LANES])
        return mrun

    def probs_chunk(qi, c0, m):
        p_ref[qi % 2, :, c0:c0 + tq] = jnp.exp2(s_ref[qi % 2, :, c0:c0 + tq] - m).astype(bf16)

    def finish(qi):
        nk = (qi + 1) * tq
        acc = jnp.dot(p_ref[qi % 2, :, 0:nk], vv_ref[0:nk, :], preferred_element_type=jnp.float32)
        denom = acc[:, LANES:LANES + 1]
        o0 = acc[0:tq, 0:LANES] / denom[0:tq]
        o1 = acc[tq:2 * tq, 0:LANES] / denom[tq:2 * tq]
        o_ref[0, qi * tq:(qi + 1) * tq, :] = jnp.where(lane < HEAD_DIM, o0, o1).astype(o_ref.dtype)

    neg = jnp.full((2 * tq, LANES), NEG_BIG, jnp.float32)
    lhs = query_operand(0)
    m = jnp.max(logits_chunk(0, 0, lhs, neg), axis=1, keepdims=True)
    for qi in range(n_q):
        m_next = None
        if qi + 1 < n_q:
            lhs = query_operand(qi + 1)
            mrun = neg
        for k in range(qi + 2):
            if qi + 1 < n_q:
                mrun = logits_chunk(qi + 1, k * tq, lhs, mrun)
            if k <= qi:
                probs_chunk(qi, k * tq, m)
        if qi + 1 < n_q:
            m_next = jnp.max(mrun, axis=1, keepdims=True)
        finish(qi)
        m = m_next

    built = 0
    for i0 in range(0, seq // SUBLANES, CONV_TILES):
        built = _conv_chunk(rot_ref, cw_ref, cb_ref, cv_ref, i0, built)


def _attn_call(q, k, v, qa, ka, u4, cwb, cbb, *, layer):
    b, s, w = q.shape
    n_pairs = w // LANES
    tiles = s // SUBLANES
    head_blk = lambda bi, p: (bi, 0, p)
    shared_blk = lambda bi, p: (bi, 0, 0)
    return pl.pallas_call(
        _attn_kernel,
        grid=(b, n_pairs),
        in_specs=[
            pl.BlockSpec((1, s, LANES), head_blk),
            pl.BlockSpec((1, s, LANES), head_blk),
            pl.BlockSpec((1, s, LANES), head_blk),
            pl.BlockSpec((1, s, LANES), shared_blk),
            pl.BlockSpec((1, s, LANES), shared_blk),
            pl.BlockSpec((1, tiles, SUBLANES, LANES), lambda bi, p: (bi, 0, 0, p)),
            pl.BlockSpec((None, CONV_K, SUBLANES, LANES), lambda bi, p: (layer, 0, 0, p)),
            pl.BlockSpec((None, SUBLANES, LANES), lambda bi, p: (layer, 0, p)),
        ],
        out_specs=[
            pl.BlockSpec((1, s, LANES), head_blk),
            pl.BlockSpec((1, tiles, SUBLANES, LANES), lambda bi, p: (bi, 0, 0, p)),
        ],
        out_shape=[
            jax.ShapeDtypeStruct((b, s, w), jnp.bfloat16),
            jax.ShapeDtypeStruct(u4.shape, jnp.float32),
        ],
        scratch_shapes=[pltpu.VMEM((s, 2 * LANES), jnp.bfloat16),
                        pltpu.VMEM((s, 2 * LANES), jnp.bfloat16),
                        pltpu.VMEM((2, 2 * TQ, s), jnp.float32),
                        pltpu.VMEM((2, 2 * TQ, s), jnp.bfloat16),
                        pltpu.VMEM((SUBLANES, HALO_TILES + tiles, SUBLANES, LANES), jnp.float32)],
        compiler_params=pltpu.CompilerParams(
            dimension_semantics=("arbitrary", "arbitrary"), vmem_limit_bytes=VMEM_LIMIT),
        name="fox_attention_conv",
    )(q, k, v, qa, ka, u4, cwb, cbb)


def _mixffn_kernel(cv_ref, attn_ref, x_ref, mod_ref, lg_ref, lb_ref, wo_ref,
                   gpost_ref, gfpre_ref, win_ref, wout_ref, gfpost_ref, xo_ref, a_ref):
    bf16 = jnp.bfloat16
    g1 = mod_ref[0, 2:3, :]
    sh2 = mod_ref[0, 3:4, :]
    sc2 = mod_ref[0, 4:5, :]
    g2 = mod_ref[0, 5:6, :]
    cv = cv_ref[...]
    mu = jnp.mean(cv, axis=-1, keepdims=True)
    cen = cv - mu
    var = jnp.mean(cen * cen, axis=-1, keepdims=True)
    yl = cen * lax.rsqrt(var + EPS) * lg_ref[...] + lb_ref[...]
    act = (yl * jax.nn.sigmoid(yl)).astype(bf16)

    y = (jnp.dot(attn_ref[...], wo_ref[0:ATTN_WIDTH, :], preferred_element_type=jnp.float32)
         + jnp.dot(act, wo_ref[ATTN_WIDTH:, :], preferred_element_type=jnp.float32))
    xn = x_ref[...] + g1 * _rms(y, gpost_ref[...])
    h = (_rms(xn, gfpre_ref[...]) * (1.0 + sc2) + sh2).astype(bf16)

    for c0 in range(0, D_FF, FF_CHUNK):
        g = jnp.dot(h, win_ref[:, c0:c0 + FF_CHUNK], preferred_element_type=jnp.float32)
        u = jnp.dot(h, win_ref[:, D_FF + c0:D_FF + c0 + FF_CHUNK], preferred_element_type=jnp.float32)
        a_ref[:, c0:c0 + FF_CHUNK] = (g * jax.nn.sigmoid(g) * u).astype(bf16)
    y2 = jnp.dot(a_ref[...], wout_ref[...], preferred_element_type=jnp.float32)
    xo_ref[...] = xn + g2 * _rms(y2, gfpost_ref[...])


def _mixffn_call(cv, attn, x2, mod, lg, lb, w_o, gpost, gfpre, w_in, w_out, gfpost, *, layer, seq):
    t, d = x2.shape
    tm = TM_FFN
    tps = seq // tm
    row = lambda i: (i, 0)
    once = pl.Buffered(1)
    resident = lambda a: pl.BlockSpec((None,) + a.shape[1:], lambda i: (layer, 0, 0), pipeline_mode=once)
    return pl.pallas_call(
        _mixffn_kernel,
        grid=(t // tm,),
        in_specs=[
            pl.BlockSpec((tm, CONV_WIDTH), row),
            pl.BlockSpec((tm, ATTN_WIDTH), row),
            pl.BlockSpec((tm, d), row),
            pl.BlockSpec((1, N_MOD, d), lambda i: (i // tps, 0, 0)),
            resident(lg), resident(lb), resident(w_o),
            resident(gpost), resident(gfpre), resident(w_in), resident(w_out), resident(gfpost),
        ],
        out_specs=pl.BlockSpec((tm, d), row),
        out_shape=jax.ShapeDtypeStruct((t, d), jnp.float32),
        scratch_shapes=[pltpu.VMEM((tm, D_FF), jnp.bfloat16)],
        compiler_params=pltpu.CompilerParams(
            dimension_semantics=("arbitrary",), vmem_limit_bytes=VMEM_LIMIT),
        name="outproj_ffn",
    )(cv, attn, x2, mod, lg, lb, w_o, gpost, gfpre, w_in, w_out, gfpost)


def _forget_weights(w_f, b_f):
    depth, d, _ = w_f.shape
    keep = (jnp.arange(AUG_STRIDE) < 6).astype(w_f.dtype)
    wf = (jnp.repeat(w_f[..., None], AUG_STRIDE, axis=-1) * keep).reshape(depth, d, N_HEADS * AUG_STRIDE)
    bf = (jnp.repeat(b_f[..., None], AUG_STRIDE, axis=-1) * keep).reshape(depth, 1, N_HEADS * AUG_STRIDE)
    return wf, bf


def kernel(x, c, w_in, b_f, conv_w, conv_b, conv_ln_g, conv_ln_b, w_o, w_ffn_in, w_ffn_out,
           mix_pre_g, mix_post_g, ffn_pre_g, ffn_post_g, ada_w, ada_b):
    b, s, d = x.shape
    depth = w_in.shape[0]
    bf16 = jnp.bfloat16
    aw, cw = ATTN_WIDTH, CONV_WIDTH

    mod_all = _ada_call(c, ada_w, ada_b).reshape(depth, b, N_MOD, d)
    tri = jnp.tri(TM_PROJ, dtype=bf16)
    x2 = x.reshape(b * s, d)

    wqkv = w_in[:, :, 0:3 * aw].astype(bf16)
    wf, bfw = _forget_weights(w_in[:, :, 3 * aw:3 * aw + N_HEADS], b_f)
    wf = wf.astype(bf16)
    wcv = w_in[:, :, 3 * aw + N_HEADS:].astype(bf16)
    w_o_b = w_o.astype(bf16)
    w_fi_b = w_ffn_in.astype(bf16)
    w_fo_b = w_ffn_out.astype(bf16)
    cwb = jnp.broadcast_to(conv_w[:, :, None, :], (depth, CONV_K, SUBLANES, cw))
    cbb = jnp.broadcast_to(conv_b[:, None, :], (depth, SUBLANES, cw))
    vec = lambda a: a.reshape(depth, 1, a.shape[-1])

    for l in range(depth):
        mod = mod_all[l]
        q, k, v, qa, ka, u = _proj_call(
            x2, mod, vec(mix_pre_g), wqkv, wf, bfw, wcv, tri, layer=l, seq=s)
        attn, cv = _attn_call(
            q.reshape(b, s, aw), k.reshape(b, s, aw), v.reshape(b, s, aw),
            qa.reshape(b, s, LANES), ka.reshape(b, s, LANES),
            u.reshape(b, s // SUBLANES, SUBLANES, cw), cwb, cbb, layer=l)
        x2 = _mixffn_call(
            cv.reshape(b * s, cw), attn.reshape(b * s, aw), x2, mod,
            vec(conv_ln_g), vec(conv_ln_b), w_o_b, vec(mix_post_g), vec(ffn_pre_g),
            w_fi_b, w_fo_b, vec(ffn_post_g), layer=l, seq=s)
    return x2.reshape(b, s, d)
```

```python
import functools
import math

import jax
import jax.numpy as jnp
from jax import lax
from jax.experimental import pallas as pl
from jax.experimental.pallas import tpu as pltpu

D_MODEL = 1024
ATTN_WIDTH = 512
HEAD_DIM = 64
N_HEADS = 8
CONV_WIDTH = 512
CONV_K = 31
D_FF = 2816
N_MOD = 6
EPS = 1e-6

LANES = 128
SUBLANES = 8
AUG_STRIDE = 16
NEG_BIG = -1e30
LOG2E = math.log2(math.e)

TM_PROJ = 512
PROJ_SUBTILES = 2
TQ = 256
TM_FFN = 1024
FF_CHUNK = 256
FFN_SUBTILES = 4
HALO_TILES = 4
CONV_TILES = 8
VMEM_LIMIT = 56 * 1024 * 1024


def _split3(x):
    hi = x.astype(jnp.bfloat16).astype(jnp.float32)
    r = x - hi
    mid = r.astype(jnp.bfloat16).astype(jnp.float32)
    lo = (r - mid).astype(jnp.bfloat16).astype(jnp.float32)
    return hi, mid, lo


def _rms(x, g):
    return x * lax.rsqrt(jnp.mean(x * x, axis=-1, keepdims=True) + EPS) * g


def _ada_kernel(c_ref, w_ref, b_ref, o_ref):
    c = c_ref[...]
    ca = (c * jax.nn.sigmoid(c)).astype(jnp.bfloat16)
    w = w_ref[0].astype(jnp.bfloat16)
    o_ref[0] = jnp.dot(ca, w, preferred_element_type=jnp.float32) + b_ref[0]


def _ada_call(c, ada_w, ada_b):
    depth, d, n = ada_w.shape
    b = c.shape[0]
    tn = 1536
    return pl.pallas_call(
        _ada_kernel,
        grid=(depth, n // tn),
        in_specs=[
            pl.BlockSpec((b, d), lambda l, j: (0, 0)),
            pl.BlockSpec((1, d, tn), lambda l, j: (l, 0, j)),
            pl.BlockSpec((1, 1, tn), lambda l, j: (l, 0, j)),
        ],
        out_specs=pl.BlockSpec((1, b, tn), lambda l, j: (l, 0, j)),
        out_shape=jax.ShapeDtypeStruct((depth, b, n), jnp.float32),
        compiler_params=pltpu.CompilerParams(
            dimension_semantics=("arbitrary", "arbitrary"), vmem_limit_bytes=VMEM_LIMIT),
        name="adaln_mod",
    )(c, ada_w, ada_b.reshape(depth, 1, n))


def _proj_kernel(x_ref, mod_ref, g_ref, wqkv_ref, wf_ref, bf_ref, wcv_ref, tri_ref,
                 q_ref, k_ref, v_ref, qa_ref, ka_ref, u_ref, carry_ref, hb_ref, *, tiles_per_seq):
    si = pl.program_id(0) % tiles_per_seq

    @pl.when(si == 0)
    def _():
        carry_ref[...] = jnp.zeros_like(carry_ref)

    sh = mod_ref[0, 0:1, :]
    sc = mod_ref[0, 1:2, :]
    aw, cw = ATTN_WIDTH, CONV_WIDTH
    sub = tri_ref.shape[0]
    tiles = [slice(r0, r0 + sub) for r0 in range(0, x_ref.shape[0], sub)]

    def pre_norm(rows):
        h = _rms(x_ref[rows, :], g_ref[...]) * (1.0 + sc) + sh
        hb_ref[rows, :] = h.astype(jnp.bfloat16)

    def qkv(rows):
        hb = hb_ref[rows, :]
        q = jnp.dot(hb, wqkv_ref[:, 0:aw], preferred_element_type=jnp.float32)
        q_ref[rows, :] = (q * (HEAD_DIM ** -0.5 * LOG2E)).astype(jnp.bfloat16)
        k_ref[rows, :] = jnp.dot(
            hb, wqkv_ref[:, aw:2 * aw], preferred_element_type=jnp.float32).astype(jnp.bfloat16)
        v_ref[rows, :] = jnp.dot(
            hb, wqkv_ref[:, 2 * aw:3 * aw], preferred_element_type=jnp.float32).astype(jnp.bfloat16)

    def decay_and_glu(rows):
        hb = hb_ref[rows, :]
        fl = jnp.dot(hb, wf_ref[...], preferred_element_type=jnp.float32) + bf_ref[...]
        val = jnp.dot(hb, wcv_ref[:, 0:cw], preferred_element_type=jnp.float32)
        gate = jnp.dot(hb, wcv_ref[:, cw:2 * cw], preferred_element_type=jnp.float32)
        u_ref[rows, :] = val * jax.nn.sigmoid(gate)
        logf = -(jnp.maximum(-fl, 0.0) + jnp.log1p(jnp.exp(-jnp.abs(fl))))
        hi, mid, lo = _split3(logf)
        parts = jnp.concatenate([hi, mid, lo], axis=1).astype(jnp.bfloat16)
        cp = jnp.dot(tri_ref[...], parts, preferred_element_type=jnp.float32)
        cum = cp[:, 0:LANES] + cp[:, LANES:2 * LANES] + cp[:, 2 * LANES:3 * LANES] + carry_ref[...]
        carry_ref[...] = cum[sub - 1:sub, :]

        chi, cmid, clo = _split3(cum * LOG2E)
        one = jnp.ones_like(chi)
        zero = jnp.zeros_like(chi)
        lane = lax.broadcasted_iota(jnp.int32, cum.shape, 1) % AUG_STRIDE
        qa_ref[rows, :] = jnp.where(lane == 0, chi, jnp.where(lane == 1, cmid, jnp.where(
            lane == 2, clo, jnp.where(lane < 6, one, zero)))).astype(qa_ref.dtype)
        ka_ref[rows, :] = jnp.where(lane < 3, one, jnp.where(lane == 3, -chi, jnp.where(
            lane == 4, -cmid, jnp.where(lane == 5, -clo, zero)))).astype(ka_ref.dtype)

    pre_norm(tiles[0])
    for t, rows in enumerate(tiles):
        qkv(rows)
        if t + 1 < len(tiles):
            pre_norm(tiles[t + 1])
        decay_and_glu(rows)


def _proj_call(x2, mod, g_pre, wqkv, wf, bfw, wcv, tri, *, layer, seq):
    t, d = x2.shape
    tm = TM_PROJ
    tps = seq // tm
    row = lambda i: (i, 0)
    bf16 = jnp.bfloat16
    of_layer = lambda a: pl.BlockSpec((None,) + a.shape[1:], lambda i: (layer, 0, 0))
    return pl.pallas_call(
        functools.partial(_proj_kernel, tiles_per_seq=tps),
        grid=(t // tm,),
        in_specs=[
            pl.BlockSpec((tm, d), row),
            pl.BlockSpec((1, N_MOD, d), lambda i: (i // tps, 0, 0)),
            of_layer(g_pre), of_layer(wqkv), of_layer(wf), of_layer(bfw), of_layer(wcv),
            pl.BlockSpec(tri.shape, lambda i: (0, 0)),
        ],
        out_specs=[
            pl.BlockSpec((tm, ATTN_WIDTH), row),
            pl.BlockSpec((tm, ATTN_WIDTH), row),
            pl.BlockSpec((tm, ATTN_WIDTH), row),
            pl.BlockSpec((tm, LANES), row),
            pl.BlockSpec((tm, LANES), row),
            pl.BlockSpec((tm, CONV_WIDTH), row),
        ],
        out_shape=[
            jax.ShapeDtypeStruct((t, ATTN_WIDTH), bf16),
            jax.ShapeDtypeStruct((t, ATTN_WIDTH), bf16),
            jax.ShapeDtypeStruct((t, ATTN_WIDTH), bf16),
            jax.ShapeDtypeStruct((t, LANES), bf16),
            jax.ShapeDtypeStruct((t, LANES), bf16),
            jax.ShapeDtypeStruct((t, CONV_WIDTH), jnp.float32),
        ],
        scratch_shapes=[pltpu.VMEM((1, LANES), jnp.float32),
                        pltpu.VMEM((tm, d), jnp.bfloat16)],
        compiler_params=pltpu.CompilerParams(
            dimension_semantics=("arbitrary",), vmem_limit_bytes=VMEM_LIMIT),
        name="in_proj",
    )(x2, mod, g_pre, wqkv, wf, bfw, wcv, tri)


def _conv_chunk(rot_ref, cw_ref, cb_ref, cv_ref, i0, built):
    n = CONV_TILES
    off0 = SUBLANES * HALO_TILES - (CONV_K - 1)
    need = i0 + n + max((j + off0) // SUBLANES for j in range(CONV_K) if (j + off0) % SUBLANES)
    if need > built:
        nb = need - built
        sub = lax.broadcasted_iota(jnp.int32, (nb, SUBLANES, LANES), 1)
        w = rot_ref[0, built:built + nb + 1]
        for rho in range(1, SUBLANES):
            wr = pltpu.roll(w, SUBLANES - rho, axis=1)
            rot_ref[rho, built:built + nb] = jnp.where(sub < SUBLANES - rho, wr[0:nb], wr[1:nb + 1])
        built = need
    acc = jnp.broadcast_to(cb_ref[...], (n, SUBLANES, LANES))
    for j in range(CONV_K):
        rho, m = (j + off0) % SUBLANES, (j + off0) // SUBLANES
        acc = acc + cw_ref[j] * rot_ref[rho, i0 + m:i0 + m + n]
    cv_ref[0, i0:i0 + n] = acc
    return built


def _attn_kernel(q_ref, k_ref, v_ref, qa_ref, ka_ref, u_ref, cw_ref, cb_ref, o_ref, cv_ref,
                 kk_ref, vv_ref, s_ref, p_ref, rot_ref):
    pair = pl.program_id(1)
    seq = q_ref.shape[1]
    tq = TQ
    n_q = seq // tq
    bf16 = jnp.bfloat16
    nt = (((1,), (1,)), ((), ()))

    kk_ref[:, 0:LANES] = k_ref[0]
    kk_ref[:, LANES:2 * LANES] = ka_ref[0]
    vv_ref[:, 0:LANES] = v_ref[0]
    vv_ref[:, LANES:2 * LANES] = jnp.ones((seq, LANES), bf16)
    rot_ref[0, 0:HALO_TILES] = jnp.zeros((HALO_TILES, SUBLANES, LANES), jnp.float32)
    rot_ref[0, HALO_TILES:] = u_ref[0]

    lane = lax.broadcasted_iota(jnp.int32, (tq, LANES), 1)
    head_of_lane = lane // AUG_STRIDE
    row = lax.broadcasted_iota(jnp.int32, (2 * tq, tq), 0) % tq
    col = lax.broadcasted_iota(jnp.int32, (2 * tq, tq), 1)
    causal = col <= row

    def query_operand(qi):
        q = q_ref[0, qi * tq:(qi + 1) * tq, :].astype(jnp.float32)
        qa = qa_ref[0, qi * tq:(qi + 1) * tq, :].astype(jnp.float32)
        zq = jnp.zeros_like(q)
        return jnp.concatenate([
            jnp.concatenate([jnp.where(lane < HEAD_DIM, q, zq),
                             jnp.where(head_of_lane == 2 * pair, qa, zq)], axis=1),
            jnp.concatenate([jnp.where(lane >= HEAD_DIM, q, zq),
                             jnp.where(head_of_lane == 2 * pair + 1, qa, zq)], axis=1),
        ], axis=0).astype(bf16)

    def logits_chunk(qi, c0, lhs, mrun):
        s = lax.dot_general(lhs, kk_ref[c0:c0 + tq, :], nt, preferred_element_type=jnp.float32)
        if c0 == qi * tq:
            s = jnp.where(causal, s, NEG_BIG)
        s_ref[qi % 2, :, c0:c0 + tq] = s
        for l0 in range(0, tq, LANES):
            mrun = jnp.maximum(mrun, s[:, l0:l0 + LANES])
        return mrun

    def probs_chunk(qi, c0, m):
        p_ref[qi % 2, :, c0:c0 + tq] = jnp.exp2(s_ref[qi % 2, :, c0:c0 + tq] - m).astype(bf16)

    def finish(qi):
        nk = (qi + 1) * tq
        acc = jnp.dot(p_ref[qi % 2, :, 0:nk], vv_ref[0:nk, :], preferred_element_type=jnp.float32)
        denom = acc[:, LANES:LANES + 1]
        o0 = acc[0:tq, 0:LANES] / denom[0:tq]
        o1 = acc[tq:2 * tq, 0:LANES] / denom[tq:2 * tq]
        o_ref[0, qi * tq:(qi + 1) * tq, :] = jnp.where(lane < HEAD_DIM, o0, o1).astype(o_ref.dtype)

    neg = jnp.full((2 * tq, LANES), NEG_BIG, jnp.float32)
    lhs = query_operand(0)
    m = jnp.max(logits_chunk(0, 0, lhs, neg), axis=1, keepdims=True)
    for qi in range(n_q):
        m_next = None
        if qi + 1 < n_q:
            lhs = query_operand(qi + 1)
            mrun = neg
        for k in range(qi + 2):
            if qi + 1 < n_q:
                mrun = logits_chunk(qi + 1, k * tq, lhs, mrun)
            if k <= qi:
                probs_chunk(qi, k * tq, m)
        if qi + 1 < n_q:
            m_next = jnp.max(mrun, axis=1, keepdims=True)
        finish(qi)
        m = m_next

    built = 0
    for i0 in range(0, seq // SUBLANES, CONV_TILES):
        built = _conv_chunk(rot_ref, cw_ref, cb_ref, cv_ref, i0, built)


def _attn_call(q, k, v, qa, ka, u4, cwb, cbb, *, layer):
    b, s, w = q.shape
    n_pairs = w // LANES
    tiles = s // SUBLANES
    head_blk = lambda bi, p: (bi, 0, p)
    shared_blk = lambda bi, p: (bi, 0, 0)
    return pl.pallas_call(
        _attn_kernel,
        grid=(b, n_pairs),
        in_specs=[
            pl.BlockSpec((1, s, LANES), head_blk),
            pl.BlockSpec((1, s, LANES), head_blk),
            pl.BlockSpec((1, s, LANES), head_blk),
            pl.BlockSpec((1, s, LANES), shared_blk),
            pl.BlockSpec((1, s, LANES), shared_blk),
            pl.BlockSpec((1, tiles, SUBLANES, LANES), lambda bi, p: (bi, 0, 0, p)),
            pl.BlockSpec((None, CONV_K, SUBLANES, LANES), lambda bi, p: (layer, 0, 0, p)),
            pl.BlockSpec((None, SUBLANES, LANES), lambda bi, p: (layer, 0, p)),
        ],
        out_specs=[
            pl.BlockSpec((1, s, LANES), head_blk),
            pl.BlockSpec((1, tiles, SUBLANES, LANES), lambda bi, p: (bi, 0, 0, p)),
        ],
        out_shape=[
            jax.ShapeDtypeStruct((b, s, w), jnp.bfloat16),
            jax.ShapeDtypeStruct(u4.shape, jnp.float32),
        ],
        scratch_shapes=[pltpu.VMEM((s, 2 * LANES), jnp.bfloat16),
                        pltpu.VMEM((s, 2 * LANES), jnp.bfloat16),
                        pltpu.VMEM((2, 2 * TQ, s), jnp.float32),
                        pltpu.VMEM((2, 2 * TQ, s), jnp.bfloat16),
                        pltpu.VMEM((SUBLANES, HALO_TILES + tiles, SUBLANES, LANES), jnp.float32)],
        compiler_params=pltpu.CompilerParams(
            dimension_semantics=("arbitrary", "arbitrary"), vmem_limit_bytes=VMEM_LIMIT),
        name="fox_attention_conv",
    )(q, k, v, qa, ka, u4, cwb, cbb)


def _mixffn_kernel(cv_ref, attn_ref, x_ref, mod_ref, lg_ref, lb_ref, wo_ref,
                   gpost_ref, gfpre_ref, win_ref, wout_ref, gfpost_ref, xo_ref, a_ref, xn_ref, h_ref):
    bf16 = jnp.bfloat16
    g1 = mod_ref[0, 2:3, :]
    sh2 = mod_ref[0, 3:4, :]
    sc2 = mod_ref[0, 4:5, :]
    g2 = mod_ref[0, 5:6, :]
    tm = x_ref.shape[0]
    sub = tm // FFN_SUBTILES
    tiles = [slice(r0, r0 + sub) for r0 in range(0, tm, sub)]

    def mixer(rows):
        cv = cv_ref[rows, :]
        mu = jnp.mean(cv, axis=-1, keepdims=True)
        cen = cv - mu
        var = jnp.mean(cen * cen, axis=-1, keepdims=True)
        yl = cen * lax.rsqrt(var + EPS) * lg_ref[...] + lb_ref[...]
        act = (yl * jax.nn.sigmoid(yl)).astype(bf16)
        y = (jnp.dot(attn_ref[rows, :], wo_ref[0:ATTN_WIDTH, :], preferred_element_type=jnp.float32)
             + jnp.dot(act, wo_ref[ATTN_WIDTH:, :], preferred_element_type=jnp.float32))
        xn = x_ref[rows, :] + g1 * _rms(y, gpost_ref[...])
        xn_ref[rows, :] = xn
        h_ref[rows, :] = (_rms(xn, gfpre_ref[...]) * (1.0 + sc2) + sh2).astype(bf16)

    def ffn_chunk(rows, c0):
        h = h_ref[rows, :]
        g = jnp.dot(h, win_ref[:, c0:c0 + FF_CHUNK], preferred_element_type=jnp.float32)
        u = jnp.dot(h, win_ref[:, D_FF + c0:D_FF + c0 + FF_CHUNK], preferred_element_type=jnp.float32)
        a_ref[rows, c0:c0 + FF_CHUNK] = (g * jax.nn.sigmoid(g) * u).astype(bf16)

    def ffn_out(rows):
        return jnp.dot(a_ref[rows, :], wout_ref[...], preferred_element_type=jnp.float32)

    def residual(rows, y2):
        xo_ref[rows, :] = xn_ref[rows, :] + g2 * _rms(y2, gfpost_ref[...])

    chunks = list(range(0, D_FF, FF_CHUNK))
    mixer(tiles[0])
    pending = None
    for t, rows in enumerate(tiles):
        ffn_chunk(rows, chunks[0])
        if pending is not None:
            residual(*pending)
        if t + 1 < len(tiles):
            mixer(tiles[t + 1])
        for c0 in chunks[1:]:
            ffn_chunk(rows, c0)
        pending = (rows, ffn_out(rows))
    residual(*pending)


def _mixffn_call(cv, attn, x2, mod, lg, lb, w_o, gpost, gfpre, w_in, w_out, gfpost, *, layer, seq):
    t, d = x2.shape
    tm = TM_FFN
    tps = seq // tm
    row = lambda i: (i, 0)
    once = pl.Buffered(1)
    resident = lambda a: pl.BlockSpec((None,) + a.shape[1:], lambda i: (layer, 0, 0), pipeline_mode=once)
    return pl.pallas_call(
        _mixffn_kernel,
        grid=(t // tm,),
        in_specs=[
            pl.BlockSpec((tm, CONV_WIDTH), row),
            pl.BlockSpec((tm, ATTN_WIDTH), row),
            pl.BlockSpec((tm, d), row),
            pl.BlockSpec((1, N_MOD, d), lambda i: (i // tps, 0, 0)),
            resident(lg), resident(lb), resident(w_o),
            resident(gpost), resident(gfpre), resident(w_in), resident(w_out), resident(gfpost),
        ],
        out_specs=pl.BlockSpec((tm, d), row),
        out_shape=jax.ShapeDtypeStruct((t, d), jnp.float32),
        scratch_shapes=[pltpu.VMEM((tm, D_FF), jnp.bfloat16),
                        pltpu.VMEM((tm, d), jnp.float32),
                        pltpu.VMEM((tm, d), jnp.bfloat16)],
        compiler_params=pltpu.CompilerParams(
            dimension_semantics=("arbitrary",), vmem_limit_bytes=VMEM_LIMIT),
        name="outproj_ffn",
    )(cv, attn, x2, mod, lg, lb, w_o, gpost, gfpre, w_in, w_out, gfpost)


def _forget_weights(w_f, b_f):
    depth, d, _ = w_f.shape
    keep = (jnp.arange(AUG_STRIDE) < 6).astype(w_f.dtype)
    wf = (jnp.repeat(w_f[..., None], AUG_STRIDE, axis=-1) * keep).reshape(depth, d, N_HEADS * AUG_STRIDE)
    bf = (jnp.repeat(b_f[..., None], AUG_STRIDE, axis=-1) * keep).reshape(depth, 1, N_HEADS * AUG_STRIDE)
    return wf, bf


def kernel(x, c, w_in, b_f, conv_w, conv_b, conv_ln_g, conv_ln_b, w_o, w_ffn_in, w_ffn_out,
           mix_pre_g, mix_post_g, ffn_pre_g, ffn_post_g, ada_w, ada_b):
    b, s, d = x.shape
    depth = w_in.shape[0]
    bf16 = jnp.bfloat16
    aw, cw = ATTN_WIDTH, CONV_WIDTH

    mod_all = _ada_call(c, ada_w, ada_b).reshape(depth, b, N_MOD, d)
    tri = jnp.tri(TM_PROJ // PROJ_SUBTILES, dtype=bf16)
    x2 = x.reshape(b * s, d)

    wqkv = w_in[:, :, 0:3 * aw].astype(bf16)
    wf, bfw = _forget_weights(w_in[:, :, 3 * aw:3 * aw + N_HEADS], b_f)
    wf = wf.astype(bf16)
    wcv = w_in[:, :, 3 * aw + N_HEADS:].astype(bf16)
    w_o_b = w_o.astype(bf16)
    w_fi_b = w_ffn_in.astype(bf16)
    w_fo_b = w_ffn_out.astype(bf16)
    cwb = jnp.broadcast_to(conv_w[:, :, None, :], (depth, CONV_K, SUBLANES, cw))
    cbb = jnp.broadcast_to(conv_b[:, None, :], (depth, SUBLANES, cw))
    vec = lambda a: a.reshape(depth, 1, a.shape[-1])

    for l in range(depth):
        mod = mod_all[l]
        q, k, v, qa, ka, u = _proj_call(
            x2, mod, vec(mix_pre_g), wqkv, wf, bfw, wcv, tri, layer=l, seq=s)
        attn, cv = _attn_call(
            q.reshape(b, s, aw), k.reshape(b, s, aw), v.reshape(b, s, aw),
            qa.reshape(b, s, LANES), ka.reshape(b, s, LANES),
            u.reshape(b, s // SUBLANES, SUBLANES, cw), cwb, cbb, layer=l)
        x2 = _mixffn_call(
            cv.reshape(b * s, cw), attn.reshape(b * s, aw), x2, mod,
            vec(conv_ln_g), vec(conv_ln_b), w_o_b, vec(mix_post_g), vec(ffn_pre_g),
            w_fi_b, w_fo_b, vec(ffn_post_g), layer=l, seq=s)
    return x2.reshape(b, s, d)
```

```python
import functools
import math

import jax
import jax.numpy as jnp
from jax import lax
from jax.experimental import pallas as pl
from jax.experimental.pallas import tpu as pltpu

D_MODEL = 1024
ATTN_WIDTH = 512
HEAD_DIM = 64
N_HEADS = 8
CONV_WIDTH = 512
CONV_K = 31
D_FF = 2816
N_MOD = 6
EPS = 1e-6

LANES = 128
SUBLANES = 8
AUG_STRIDE = 16
NEG_BIG = -1e30
LOG2E = math.log2(math.e)

TM_PROJ = 512
PROJ_SUBTILES = 2
TQ = 256
TM_FFN = 1024
FF_CHUNK = 256
FFN_SUBTILES = 4
HALO_TILES = 4
CONV_TILES = 8
VMEM_LIMIT = 56 * 1024 * 1024


def _split3(x):
    hi = x.astype(jnp.bfloat16).astype(jnp.float32)
    r = x - hi
    mid = r.astype(jnp.bfloat16).astype(jnp.float32)
    lo = (r - mid).astype(jnp.bfloat16).astype(jnp.float32)
    return hi, mid, lo


def _rms(x, g):
    return x * lax.rsqrt(jnp.mean(x * x, axis=-1, keepdims=True) + EPS) * g


def _ada_kernel(c_ref, w_ref, b_ref, o_ref):
    c = c_ref[...]
    ca = (c * jax.nn.sigmoid(c)).astype(jnp.bfloat16)
    w = w_ref[0].astype(jnp.bfloat16)
    o_ref[0] = jnp.dot(ca, w, preferred_element_type=jnp.float32) + b_ref[0]


def _ada_call(c, ada_w, ada_b):
    depth, d, n = ada_w.shape
    b = c.shape[0]
    tn = 1536
    return pl.pallas_call(
        _ada_kernel,
        grid=(depth, n // tn),
        in_specs=[
            pl.BlockSpec((b, d), lambda l, j: (0, 0)),
            pl.BlockSpec((1, d, tn), lambda l, j: (l, 0, j)),
            pl.BlockSpec((1, 1, tn), lambda l, j: (l, 0, j)),
        ],
        out_specs=pl.BlockSpec((1, b, tn), lambda l, j: (l, 0, j)),
        out_shape=jax.ShapeDtypeStruct((depth, b, n), jnp.float32),
        compiler_params=pltpu.CompilerParams(
            dimension_semantics=("arbitrary", "arbitrary"), vmem_limit_bytes=VMEM_LIMIT),
        name="adaln_mod",
    )(c, ada_w, ada_b.reshape(depth, 1, n))


def _proj_kernel(x_ref, mod_ref, g_ref, wqkv_ref, wf_ref, bf_ref, wcv_ref, tri_ref,
                 q_ref, k_ref, v_ref, qa_ref, ka_ref, u_ref, carry_ref, hb_ref, *, tiles_per_seq):
    si = pl.program_id(0) % tiles_per_seq

    @pl.when(si == 0)
    def _():
        carry_ref[...] = jnp.zeros_like(carry_ref)

    sh = mod_ref[0, 0:1, :]
    sc = mod_ref[0, 1:2, :]
    aw, cw = ATTN_WIDTH, CONV_WIDTH
    sub = tri_ref.shape[0]
    tiles = [slice(r0, r0 + sub) for r0 in range(0, x_ref.shape[0], sub)]

    def pre_norm(rows):
        h = _rms(x_ref[rows, :], g_ref[...]) * (1.0 + sc) + sh
        hb_ref[rows, :] = h.astype(jnp.bfloat16)

    def qkv(rows):
        hb = hb_ref[rows, :]
        q = jnp.dot(hb, wqkv_ref[:, 0:aw], preferred_element_type=jnp.float32)
        q_ref[rows, :] = (q * (HEAD_DIM ** -0.5 * LOG2E)).astype(jnp.bfloat16)
        k_ref[rows, :] = jnp.dot(
            hb, wqkv_ref[:, aw:2 * aw], preferred_element_type=jnp.float32).astype(jnp.bfloat16)
        v_ref[rows, :] = jnp.dot(
            hb, wqkv_ref[:, 2 * aw:3 * aw], preferred_element_type=jnp.float32).astype(jnp.bfloat16)

    def decay_and_glu(rows):
        hb = hb_ref[rows, :]
        fl = jnp.dot(hb, wf_ref[...], preferred_element_type=jnp.float32) + bf_ref[...]
        val = jnp.dot(hb, wcv_ref[:, 0:cw], preferred_element_type=jnp.float32)
        gate = jnp.dot(hb, wcv_ref[:, cw:2 * cw], preferred_element_type=jnp.float32)
        u_ref[rows, :] = val * jax.nn.sigmoid(gate)
        logf = -(jnp.maximum(-fl, 0.0) + jnp.log1p(jnp.exp(-jnp.abs(fl))))
        hi, mid, lo = _split3(logf)
        parts = jnp.concatenate([hi, mid, lo], axis=1).astype(jnp.bfloat16)
        cp = jnp.dot(tri_ref[...], parts, preferred_element_type=jnp.float32)
        cum = cp[:, 0:LANES] + cp[:, LANES:2 * LANES] + cp[:, 2 * LANES:3 * LANES] + carry_ref[...]
        carry_ref[...] = cum[sub - 1:sub, :]

        chi, cmid, clo = _split3(cum * LOG2E)
        one = jnp.ones_like(chi)
        zero = jnp.zeros_like(chi)
        lane = lax.broadcasted_iota(jnp.int32, cum.shape, 1) % AUG_STRIDE
        qa_ref[rows, :] = jnp.where(lane == 0, chi, jnp.where(lane == 1, cmid, jnp.where(
            lane == 2, clo, jnp.where(lane < 6, one, zero)))).astype(qa_ref.dtype)
        ka_ref[rows, :] = jnp.where(lane < 3, one, jnp.where(lane == 3, -chi, jnp.where(
            lane == 4, -cmid, jnp.where(lane == 5, -clo, zero)))).astype(ka_ref.dtype)

    pre_norm(tiles[0])
    for t, rows in enumerate(tiles):
        qkv(rows)
        if t + 1 < len(tiles):
            pre_norm(tiles[t + 1])
        decay_and_glu(rows)


def _proj_call(x2, mod, g_pre, wqkv, wf, bfw, wcv, tri, *, layer, seq):
    t, d = x2.shape
    tm = TM_PROJ
    tps = seq // tm
    row = lambda i: (i, 0)
    bf16 = jnp.bfloat16
    of_layer = lambda a: pl.BlockSpec((None,) + a.shape[1:], lambda i: (layer, 0, 0))
    return pl.pallas_call(
        functools.partial(_proj_kernel, tiles_per_seq=tps),
        grid=(t // tm,),
        in_specs=[
            pl.BlockSpec((tm, d), row),
            pl.BlockSpec((1, N_MOD, d), lambda i: (i // tps, 0, 0)),
            of_layer(g_pre), of_layer(wqkv), of_layer(wf), of_layer(bfw), of_layer(wcv),
            pl.BlockSpec(tri.shape, lambda i: (0, 0)),
        ],
        out_specs=[
            pl.BlockSpec((tm, ATTN_WIDTH), row),
            pl.BlockSpec((tm, ATTN_WIDTH), row),
            pl.BlockSpec((tm, ATTN_WIDTH), row),
            pl.BlockSpec((tm, LANES), row),
            pl.BlockSpec((tm, LANES), row),
            pl.BlockSpec((tm, CONV_WIDTH), row),
        ],
        out_shape=[
            jax.ShapeDtypeStruct((t, ATTN_WIDTH), bf16),
            jax.ShapeDtypeStruct((t, ATTN_WIDTH), bf16),
            jax.ShapeDtypeStruct((t, ATTN_WIDTH), bf16),
            jax.ShapeDtypeStruct((t, LANES), bf16),
            jax.ShapeDtypeStruct((t, LANES), bf16),
            jax.ShapeDtypeStruct((t, CONV_WIDTH), jnp.float32),
        ],
        scratch_shapes=[pltpu.VMEM((1, LANES), jnp.float32),
                        pltpu.VMEM((tm, d), jnp.bfloat16)],
        compiler_params=pltpu.CompilerParams(
            dimension_semantics=("arbitrary",), vmem_limit_bytes=VMEM_LIMIT),
        name="in_proj",
    )(x2, mod, g_pre, wqkv, wf, bfw, wcv, tri)


def _conv_chunk(rot_ref, cw_ref, cb_ref, cv_ref, i0, built):
    n = CONV_TILES
    off0 = SUBLANES * HALO_TILES - (CONV_K - 1)
    need = i0 + n + max((j + off0) // SUBLANES for j in range(CONV_K) if (j + off0) % SUBLANES)
    if need > built:
        nb = need - built
        sub = lax.broadcasted_iota(jnp.int32, (nb, SUBLANES, LANES), 1)
        w = rot_ref[0, built:built + nb + 1]
        for rho in range(1, SUBLANES):
            wr = pltpu.roll(w, SUBLANES - rho, axis=1)
            rot_ref[rho, built:built + nb] = jnp.where(sub < SUBLANES - rho, wr[0:nb], wr[1:nb + 1])
        built = need
    acc = jnp.broadcast_to(cb_ref[...], (n, SUBLANES, LANES))
    for j in range(CONV_K):
        rho, m = (j + off0) % SUBLANES, (j + off0) // SUBLANES
        acc = acc + cw_ref[j] * rot_ref[rho, i0 + m:i0 + m + n]
    cv_ref[0, i0:i0 + n] = acc
    return built


def _attn_kernel(q_ref, k_ref, v_ref, qa_ref, ka_ref, u_ref, cw_ref, cb_ref, o_ref, cv_ref,
                 kk_ref, vv_ref, s_ref, p_ref, rot_ref):
    pair = pl.program_id(1)
    seq = q_ref.shape[1]
    tq = TQ
    n_q = seq // tq
    bf16 = jnp.bfloat16
    nt = (((1,), (1,)), ((), ()))

    kk_ref[:, 0:LANES] = k_ref[0]
    kk_ref[:, LANES:2 * LANES] = ka_ref[0]
    vv_ref[:, 0:LANES] = v_ref[0]
    vv_ref[:, LANES:2 * LANES] = jnp.ones((seq, LANES), bf16)
    rot_ref[0, 0:HALO_TILES] = jnp.zeros((HALO_TILES, SUBLANES, LANES), jnp.float32)
    rot_ref[0, HALO_TILES:] = u_ref[0]

    lane = lax.broadcasted_iota(jnp.int32, (tq, LANES), 1)
    head_of_lane = lane // AUG_STRIDE
    row = lax.broadcasted_iota(jnp.int32, (2 * tq, tq), 0) % tq
    col = lax.broadcasted_iota(jnp.int32, (2 * tq, tq), 1)
    causal = col <= row

    def query_operand(qi):
        q = q_ref[0, qi * tq:(qi + 1) * tq, :].astype(jnp.float32)
        qa = qa_ref[0, qi * tq:(qi + 1) * tq, :].astype(jnp.float32)
        zq = jnp.zeros_like(q)
        return jnp.concatenate([
            jnp.concatenate([jnp.where(lane < HEAD_DIM, q, zq),
                             jnp.where(head_of_lane == 2 * pair, qa, zq)], axis=1),
            jnp.concatenate([jnp.where(lane >= HEAD_DIM, q, zq),
                             jnp.where(head_of_lane == 2 * pair + 1, qa, zq)], axis=1),
        ], axis=0).astype(bf16)

    def logits_chunk(qi, c0, lhs, mrun):
        s = lax.dot_general(lhs, kk_ref[c0:c0 + tq, :], nt, preferred_element_type=jnp.float32)
        if c0 == qi * tq:
            s = jnp.where(causal, s, NEG_BIG)
        s_ref[qi % 2, :, c0:c0 + tq] = s
        for l0 in range(0, tq, LANES):
            mrun = jnp.maximum(mrun, s[:, l0:l0 + LANES])
        return mrun

    def probs_chunk(qi, c0, m):
        p_ref[qi % 2, :, c0:c0 + tq] = jnp.exp2(s_ref[qi % 2, :, c0:c0 + tq] - m).astype(bf16)

    def finish(qi):
        nk = (qi + 1) * tq
        acc = jnp.dot(p_ref[qi % 2, :, 0:nk], vv_ref[0:nk, :], preferred_element_type=jnp.float32)
        denom = acc[:, LANES:LANES + 1]
        o0 = acc[0:tq, 0:LANES] / denom[0:tq]
        o1 = acc[tq:2 * tq, 0:LANES] / denom[tq:2 * tq]
        o_ref[0, qi * tq:(qi + 1) * tq, :] = jnp.where(lane < HEAD_DIM, o0, o1).astype(o_ref.dtype)

    neg = jnp.full((2 * tq, LANES), NEG_BIG, jnp.float32)
    lhs = query_operand(n_q - 1)
    mrun = neg
    for k in range(n_q):
        mrun = logits_chunk(n_q - 1, k * tq, lhs, mrun)
    m = jnp.max(mrun, axis=1, keepdims=True)
    for qi in range(n_q - 1, -1, -1):
        m_next = None
        if qi > 0:
            lhs = query_operand(qi - 1)
            mrun = neg
        for k in range(qi + 1):
            if k < qi:
                mrun = logits_chunk(qi - 1, k * tq, lhs, mrun)
            probs_chunk(qi, k * tq, m)
        if qi > 0:
            m_next = jnp.max(mrun, axis=1, keepdims=True)
        finish(qi)
        m = m_next

    built = 0
    for i0 in range(0, seq // SUBLANES, CONV_TILES):
        built = _conv_chunk(rot_ref, cw_ref, cb_ref, cv_ref, i0, built)


def _attn_call(q, k, v, qa, ka, u4, cwb, cbb, *, layer):
    b, s, w = q.shape
    n_pairs = w // LANES
    tiles = s // SUBLANES
    head_blk = lambda bi, p: (bi, 0, p)
    shared_blk = lambda bi, p: (bi, 0, 0)
    return pl.pallas_call(
        _attn_kernel,
        grid=(b, n_pairs),
        in_specs=[
            pl.BlockSpec((1, s, LANES), head_blk),
            pl.BlockSpec((1, s, LANES), head_blk),
            pl.BlockSpec((1, s, LANES), head_blk),
            pl.BlockSpec((1, s, LANES), shared_blk),
            pl.BlockSpec((1, s, LANES), shared_blk),
            pl.BlockSpec((1, tiles, SUBLANES, LANES), lambda bi, p: (bi, 0, 0, p)),
            pl.BlockSpec((None, CONV_K, SUBLANES, LANES), lambda bi, p: (layer, 0, 0, p)),
            pl.BlockSpec((None, SUBLANES, LANES), lambda bi, p: (layer, 0, p)),
        ],
        out_specs=[
            pl.BlockSpec((1, s, LANES), head_blk),
            pl.BlockSpec((1, tiles, SUBLANES, LANES), lambda bi, p: (bi, 0, 0, p)),
        ],
        out_shape=[
            jax.ShapeDtypeStruct((b, s, w), jnp.bfloat16),
            jax.ShapeDtypeStruct(u4.shape, jnp.float32),
        ],
        scratch_shapes=[pltpu.VMEM((s, 2 * LANES), jnp.bfloat16),
                        pltpu.VMEM((s, 2 * LANES), jnp.bfloat16),
                        pltpu.VMEM((2, 2 * TQ, s), jnp.float32),
                        pltpu.VMEM((2, 2 * TQ, s), jnp.bfloat16),
                        pltpu.VMEM((SUBLANES, HALO_TILES + tiles, SUBLANES, LANES), jnp.float32)],
        compiler_params=pltpu.CompilerParams(
            dimension_semantics=("arbitrary", "arbitrary"), vmem_limit_bytes=VMEM_LIMIT),
        name="fox_attention_conv",
    )(q, k, v, qa, ka, u4, cwb, cbb)


def _mixffn_kernel(cv_ref, attn_ref, x_ref, mod_ref, lg_ref, lb_ref, wo_ref,
                   gpost_ref, gfpre_ref, win_ref, wout_ref, gfpost_ref, xo_ref, a_ref, xn_ref, h_ref):
    bf16 = jnp.bfloat16
    g1 = mod_ref[0, 2:3, :]
    sh2 = mod_ref[0, 3:4, :]
    sc2 = mod_ref[0, 4:5, :]
    g2 = mod_ref[0, 5:6, :]
    tm = x_ref.shape[0]
    sub = tm // FFN_SUBTILES
    tiles = [slice(r0, r0 + sub) for r0 in range(0, tm, sub)]

    def mixer(rows):
        cv = cv_ref[rows, :]
        mu = jnp.mean(cv, axis=-1, keepdims=True)
        cen = cv - mu
        var = jnp.mean(cen * cen, axis=-1, keepdims=True)
        yl = cen * lax.rsqrt(var + EPS) * lg_ref[...] + lb_ref[...]
        act = (yl * jax.nn.sigmoid(yl)).astype(bf16)
        y = (jnp.dot(attn_ref[rows, :], wo_ref[0:ATTN_WIDTH, :], preferred_element_type=jnp.float32)
             + jnp.dot(act, wo_ref[ATTN_WIDTH:, :], preferred_element_type=jnp.float32))
        xn = x_ref[rows, :] + g1 * _rms(y, gpost_ref[...])
        xn_ref[rows, :] = xn
        h_ref[rows, :] = (_rms(xn, gfpre_ref[...]) * (1.0 + sc2) + sh2).astype(bf16)

    def ffn_chunk(rows, c0):
        h = h_ref[rows, :]
        g = jnp.dot(h, win_ref[:, c0:c0 + FF_CHUNK], preferred_element_type=jnp.float32)
        u = jnp.dot(h, win_ref[:, D_FF + c0:D_FF + c0 + FF_CHUNK], preferred_element_type=jnp.float32)
        a_ref[rows, c0:c0 + FF_CHUNK] = (g * jax.nn.sigmoid(g) * u).astype(bf16)

    def ffn_out(rows):
        return jnp.dot(a_ref[rows, :], wout_ref[...], preferred_element_type=jnp.float32)

    def residual(rows, y2):
        xo_ref[rows, :] = xn_ref[rows, :] + g2 * _rms(y2, gfpost_ref[...])

    chunks = list(range(0, D_FF, FF_CHUNK))
    mixer(tiles[0])
    pending = None
    for t, rows in enumerate(tiles):
        ffn_chunk(rows, chunks[0])
        if pending is not None:
            residual(*pending)
        if t + 1 < len(tiles):
            mixer(tiles[t + 1])
        for c0 in chunks[1:]:
            ffn_chunk(rows, c0)
        pending = (rows, ffn_out(rows))
    residual(*pending)


def _mixffn_call(cv, attn, x2, mod, lg, lb, w_o, gpost, gfpre, w_in, w_out, gfpost, *, layer, seq):
    t, d = x2.shape
    tm = TM_FFN
    tps = seq // tm
    row = lambda i: (i, 0)
    once = pl.Buffered(1)
    resident = lambda a: pl.BlockSpec((None,) + a.shape[1:], lambda i: (layer, 0, 0), pipeline_mode=once)
    return pl.pallas_call(
        _mixffn_kernel,
        grid=(t // tm,),
        in_specs=[
            pl.BlockSpec((tm, CONV_WIDTH), row),
            pl.BlockSpec((tm, ATTN_WIDTH), row),
            pl.BlockSpec((tm, d), row),
            pl.BlockSpec((1, N_MOD, d), lambda i: (i // tps, 0, 0)),
            resident(lg), resident(lb), resident(w_o),
            resident(gpost), resident(gfpre), resident(w_in), resident(w_out), resident(gfpost),
        ],
        out_specs=pl.BlockSpec((tm, d), row),
        out_shape=jax.ShapeDtypeStruct((t, d), jnp.float32),
        scratch_shapes=[pltpu.VMEM((tm, D_FF), jnp.bfloat16),
                        pltpu.VMEM((tm, d), jnp.float32),
                        pltpu.VMEM((tm, d), jnp.bfloat16)],
        compiler_params=pltpu.CompilerParams(
            dimension_semantics=("arbitrary",), vmem_limit_bytes=VMEM_LIMIT),
        name="outproj_ffn",
    )(cv, attn, x2, mod, lg, lb, w_o, gpost, gfpre, w_in, w_out, gfpost)


def _forget_weights(w_f, b_f):
    depth, d, _ = w_f.shape
    keep = (jnp.arange(AUG_STRIDE) < 6).astype(w_f.dtype)
    wf = (jnp.repeat(w_f[..., None], AUG_STRIDE, axis=-1) * keep).reshape(depth, d, N_HEADS * AUG_STRIDE)
    bf = (jnp.repeat(b_f[..., None], AUG_STRIDE, axis=-1) * keep).reshape(depth, 1, N_HEADS * AUG_STRIDE)
    return wf, bf


def kernel(x, c, w_in, b_f, conv_w, conv_b, conv_ln_g, conv_ln_b, w_o, w_ffn_in, w_ffn_out,
           mix_pre_g, mix_post_g, ffn_pre_g, ffn_post_g, ada_w, ada_b):
    b, s, d = x.shape
    depth = w_in.shape[0]
    bf16 = jnp.bfloat16
    aw, cw = ATTN_WIDTH, CONV_WIDTH

    mod_all = _ada_call(c, ada_w, ada_b).reshape(depth, b, N_MOD, d)
    tri = jnp.tri(TM_PROJ // PROJ_SUBTILES, dtype=bf16)
    x2 = x.reshape(b * s, d)

    wqkv = w_in[:, :, 0:3 * aw].astype(bf16)
    wf, bfw = _forget_weights(w_in[:, :, 3 * aw:3 * aw + N_HEADS], b_f)
    wf = wf.astype(bf16)
    wcv = w_in[:, :, 3 * aw + N_HEADS:].astype(bf16)
    w_o_b = w_o.astype(bf16)
    w_fi_b = w_ffn_in.astype(bf16)
    w_fo_b = w_ffn_out.astype(bf16)
    cwb = jnp.broadcast_to(conv_w[:, :, None, :], (depth, CONV_K, SUBLANES, cw))
    cbb = jnp.broadcast_to(conv_b[:, None, :], (depth, SUBLANES, cw))
    vec = lambda a: a.reshape(depth, 1, a.shape[-1])

    for l in range(depth):
        mod = mod_all[l]
        q, k, v, qa, ka, u = _proj_call(
            x2, mod, vec(mix_pre_g), wqkv, wf, bfw, wcv, tri, layer=l, seq=s)
        attn, cv = _attn_call(
            q.reshape(b, s, aw), k.reshape(b, s, aw), v.reshape(b, s, aw),
            qa.reshape(b, s, LANES), ka.reshape(b, s, LANES),
            u.reshape(b, s // SUBLANES, SUBLANES, cw), cwb, cbb, layer=l)
        x2 = _mixffn_call(
            cv.reshape(b * s, cw), attn.reshape(b * s, aw), x2, mod,
            vec(conv_ln_g), vec(conv_ln_b), w_o_b, vec(mix_post_g), vec(ffn_pre_g),
            w_fi_b, w_fo_b, vec(ffn_post_g), layer=l, seq=s)
    return x2.reshape(b, s, d)
```

```python
import functools
import math

import jax
import jax.numpy as jnp
from jax import lax
from jax.experimental import pallas as pl
from jax.experimental.pallas import tpu as pltpu

D_MODEL = 1024
ATTN_WIDTH = 512
HEAD_DIM = 64
N_HEADS = 8
CONV_WIDTH = 512
CONV_K = 31
D_FF = 2816
N_MOD = 6
EPS = 1e-6

LANES = 128
SUBLANES = 8
AUG_STRIDE = 16
NEG_BIG = -1e30
LOG2E = math.log2(math.e)

TM_PROJ = 1024
PROJ_SUBTILES = 4
TQ = 256
TM_FFN = 1024
FF_CHUNK = 256
FFN_SUBTILES = 4
HALO_TILES = 4
CONV_TILES = 8
VMEM_LIMIT = 56 * 1024 * 1024


def _split3(x):
    hi = x.astype(jnp.bfloat16).astype(jnp.float32)
    r = x - hi
    mid = r.astype(jnp.bfloat16).astype(jnp.float32)
    lo = (r - mid).astype(jnp.bfloat16).astype(jnp.float32)
    return hi, mid, lo


def _rms(x, g):
    return x * lax.rsqrt(jnp.mean(x * x, axis=-1, keepdims=True) + EPS) * g


def _ada_kernel(c_ref, w_ref, b_ref, o_ref):
    c = c_ref[...]
    ca = (c * jax.nn.sigmoid(c)).astype(jnp.bfloat16)
    w = w_ref[0].astype(jnp.bfloat16)
    o_ref[0] = jnp.dot(ca, w, preferred_element_type=jnp.float32) + b_ref[0]


def _ada_call(c, ada_w, ada_b):
    depth, d, n = ada_w.shape
    b = c.shape[0]
    tn = 1536
    return pl.pallas_call(
        _ada_kernel,
        grid=(depth, n // tn),
        in_specs=[
            pl.BlockSpec((b, d), lambda l, j: (0, 0)),
            pl.BlockSpec((1, d, tn), lambda l, j: (l, 0, j)),
            pl.BlockSpec((1, 1, tn), lambda l, j: (l, 0, j)),
        ],
        out_specs=pl.BlockSpec((1, b, tn), lambda l, j: (l, 0, j)),
        out_shape=jax.ShapeDtypeStruct((depth, b, n), jnp.float32),
        compiler_params=pltpu.CompilerParams(
            dimension_semantics=("arbitrary", "arbitrary"), vmem_limit_bytes=VMEM_LIMIT),
        name="adaln_mod",
    )(c, ada_w, ada_b.reshape(depth, 1, n))


def _proj_kernel(x_ref, mod_ref, g_ref, wqkv_ref, wf_ref, bf_ref, wcv_ref, tri_ref,
                 q_ref, k_ref, v_ref, qa_ref, ka_ref, u_ref, carry_ref, hb_ref, *, tiles_per_seq):
    si = pl.program_id(0) % tiles_per_seq

    @pl.when(si == 0)
    def _():
        carry_ref[...] = jnp.zeros_like(carry_ref)

    sh = mod_ref[0, 0:1, :]
    sc = mod_ref[0, 1:2, :]
    aw, cw = ATTN_WIDTH, CONV_WIDTH
    sub = tri_ref.shape[0]
    tiles = [slice(r0, r0 + sub) for r0 in range(0, x_ref.shape[0], sub)]

    def pre_norm(rows):
        h = _rms(x_ref[rows, :], g_ref[...]) * (1.0 + sc) + sh
        hb_ref[rows, :] = h.astype(jnp.bfloat16)

    def qkv(rows):
        hb = hb_ref[rows, :]
        q = jnp.dot(hb, wqkv_ref[:, 0:aw], preferred_element_type=jnp.float32)
        q_ref[rows, :] = (q * (HEAD_DIM ** -0.5 * LOG2E)).astype(jnp.bfloat16)
        k_ref[rows, :] = jnp.dot(
            hb, wqkv_ref[:, aw:2 * aw], preferred_element_type=jnp.float32).astype(jnp.bfloat16)
        v_ref[rows, :] = jnp.dot(
            hb, wqkv_ref[:, 2 * aw:3 * aw], preferred_element_type=jnp.float32).astype(jnp.bfloat16)

    def decay_and_glu(rows):
        hb = hb_ref[rows, :]
        fl = jnp.dot(hb, wf_ref[...], preferred_element_type=jnp.float32) + bf_ref[...]
        val = jnp.dot(hb, wcv_ref[:, 0:cw], preferred_element_type=jnp.float32)
        gate = jnp.dot(hb, wcv_ref[:, cw:2 * cw], preferred_element_type=jnp.float32)
        u_ref[rows, :] = val * jax.nn.sigmoid(gate)
        logf = -(jnp.maximum(-fl, 0.0) + jnp.log1p(jnp.exp(-jnp.abs(fl))))
        hi, mid, lo = _split3(logf)
        parts = jnp.concatenate([hi, mid, lo], axis=1).astype(jnp.bfloat16)
        cp = jnp.dot(tri_ref[...], parts, preferred_element_type=jnp.float32)
        cum = cp[:, 0:LANES] + cp[:, LANES:2 * LANES] + cp[:, 2 * LANES:3 * LANES] + carry_ref[...]
        carry_ref[...] = cum[sub - 1:sub, :]

        chi, cmid, clo = _split3(cum * LOG2E)
        one = jnp.ones_like(chi)
        zero = jnp.zeros_like(chi)
        lane = lax.broadcasted_iota(jnp.int32, cum.shape, 1) % AUG_STRIDE
        qa_ref[rows, :] = jnp.where(lane == 0, chi, jnp.where(lane == 1, cmid, jnp.where(
            lane == 2, clo, jnp.where(lane < 6, one, zero)))).astype(qa_ref.dtype)
        ka_ref[rows, :] = jnp.where(lane < 3, one, jnp.where(lane == 3, -chi, jnp.where(
            lane == 4, -cmid, jnp.where(lane == 5, -clo, zero)))).astype(ka_ref.dtype)

    pre_norm(tiles[0])
    for t, rows in enumerate(tiles):
        qkv(rows)
        if t + 1 < len(tiles):
            pre_norm(tiles[t + 1])
        decay_and_glu(rows)


def _proj_call(x2, mod, g_pre, wqkv, wf, bfw, wcv, tri, *, layer, seq):
    t, d = x2.shape
    tm = TM_PROJ
    tps = seq // tm
    row = lambda i: (i, 0)
    bf16 = jnp.bfloat16
    of_layer = lambda a: pl.BlockSpec((None,) + a.shape[1:], lambda i: (layer, 0, 0))
    return pl.pallas_call(
        functools.partial(_proj_kernel, tiles_per_seq=tps),
        grid=(t // tm,),
        in_specs=[
            pl.BlockSpec((tm, d), row),
            pl.BlockSpec((1, N_MOD, d), lambda i: (i // tps, 0, 0)),
            of_layer(g_pre), of_layer(wqkv), of_layer(wf), of_layer(bfw), of_layer(wcv),
            pl.BlockSpec(tri.shape, lambda i: (0, 0)),
        ],
        out_specs=[
            pl.BlockSpec((tm, ATTN_WIDTH), row),
            pl.BlockSpec((tm, ATTN_WIDTH), row),
            pl.BlockSpec((tm, ATTN_WIDTH), row),
            pl.BlockSpec((tm, LANES), row),
            pl.BlockSpec((tm, LANES), row),
            pl.BlockSpec((tm, CONV_WIDTH), row),
        ],
        out_shape=[
            jax.ShapeDtypeStruct((t, ATTN_WIDTH), bf16),
            jax.ShapeDtypeStruct((t, ATTN_WIDTH), bf16),
            jax.ShapeDtypeStruct((t, ATTN_WIDTH), bf16),
            jax.ShapeDtypeStruct((t, LANES), bf16),
            jax.ShapeDtypeStruct((t, LANES), bf16),
            jax.ShapeDtypeStruct((t, CONV_WIDTH), jnp.float32),
        ],
        scratch_shapes=[pltpu.VMEM((1, LANES), jnp.float32),
                        pltpu.VMEM((tm, d), jnp.bfloat16)],
        compiler_params=pltpu.CompilerParams(
            dimension_semantics=("arbitrary",), vmem_limit_bytes=VMEM_LIMIT),
        name="in_proj",
    )(x2, mod, g_pre, wqkv, wf, bfw, wcv, tri)


def _conv_chunk(rot_ref, cw_ref, cb_ref, cv_ref, i0, built):
    n = CONV_TILES
    off0 = SUBLANES * HALO_TILES - (CONV_K - 1)
    need = i0 + n + max((j + off0) // SUBLANES for j in range(CONV_K) if (j + off0) % SUBLANES)
    if need > built:
        nb = need - built
        sub = lax.broadcasted_iota(jnp.int32, (nb, SUBLANES, LANES), 1)
        w = rot_ref[0, built:built + nb + 1]
        for rho in range(1, SUBLANES):
            wr = pltpu.roll(w, SUBLANES - rho, axis=1)
            rot_ref[rho, built:built + nb] = jnp.where(sub < SUBLANES - rho, wr[0:nb], wr[1:nb + 1])
        built = need
    acc = jnp.broadcast_to(cb_ref[...], (n, SUBLANES, LANES))
    for j in range(CONV_K):
        rho, m = (j + off0) % SUBLANES, (j + off0) // SUBLANES
        acc = acc + cw_ref[j] * rot_ref[rho, i0 + m:i0 + m + n]
    cv_ref[0, i0:i0 + n] = acc
    return built


def _attn_kernel(q_ref, k_ref, v_ref, qa_ref, ka_ref, u_ref, cw_ref, cb_ref, o_ref, cv_ref,
                 kk_ref, vv_ref, s_ref, p_ref, rot_ref):
    pair = pl.program_id(1)
    seq = q_ref.shape[1]
    tq = TQ
    n_q = seq // tq
    bf16 = jnp.bfloat16
    nt = (((1,), (1,)), ((), ()))

    kk_ref[:, 0:LANES] = k_ref[0]
    kk_ref[:, LANES:2 * LANES] = ka_ref[0]
    vv_ref[:, 0:LANES] = v_ref[0]
    vv_ref[:, LANES:2 * LANES] = jnp.ones((seq, LANES), bf16)
    rot_ref[0, 0:HALO_TILES] = jnp.zeros((HALO_TILES, SUBLANES, LANES), jnp.float32)
    rot_ref[0, HALO_TILES:] = u_ref[0]

    lane = lax.broadcasted_iota(jnp.int32, (tq, LANES), 1)
    head_of_lane = lane // AUG_STRIDE
    row = lax.broadcasted_iota(jnp.int32, (2 * tq, tq), 0) % tq
    col = lax.broadcasted_iota(jnp.int32, (2 * tq, tq), 1)
    causal = col <= row

    def query_operand(qi):
        q = q_ref[0, qi * tq:(qi + 1) * tq, :].astype(jnp.float32)
        qa = qa_ref[0, qi * tq:(qi + 1) * tq, :].astype(jnp.float32)
        zq = jnp.zeros_like(q)
        return jnp.concatenate([
            jnp.concatenate([jnp.where(lane < HEAD_DIM, q, zq),
                             jnp.where(head_of_lane == 2 * pair, qa, zq)], axis=1),
            jnp.concatenate([jnp.where(lane >= HEAD_DIM, q, zq),
                             jnp.where(head_of_lane == 2 * pair + 1, qa, zq)], axis=1),
        ], axis=0).astype(bf16)

    def logits_chunk(qi, c0, lhs, mrun):
        s = lax.dot_general(lhs, kk_ref[c0:c0 + tq, :], nt, preferred_element_type=jnp.float32)
        if c0 == qi * tq:
            s = jnp.where(causal, s, NEG_BIG)
        s_ref[qi % 2, :, c0:c0 + tq] = s
        for l0 in range(0, tq, LANES):
            mrun = jnp.maximum(mrun, s[:, l0:l0 + LANES])
        return mrun

    def probs_chunk(qi, c0, m):
        p_ref[qi % 2, :, c0:c0 + tq] = jnp.exp2(s_ref[qi % 2, :, c0:c0 + tq] - m).astype(bf16)

    def finish(qi):
        nk = (qi + 1) * tq
        acc = jnp.dot(p_ref[qi % 2, :, 0:nk], vv_ref[0:nk, :], preferred_element_type=jnp.float32)
        denom = acc[:, LANES:LANES + 1]
        o0 = acc[0:tq, 0:LANES] / denom[0:tq]
        o1 = acc[tq:2 * tq, 0:LANES] / denom[tq:2 * tq]
        o_ref[0, qi * tq:(qi + 1) * tq, :] = jnp.where(lane < HEAD_DIM, o0, o1).astype(o_ref.dtype)

    neg = jnp.full((2 * tq, LANES), NEG_BIG, jnp.float32)
    lhs = query_operand(n_q - 1)
    mrun = neg
    for k in range(n_q):
        mrun = logits_chunk(n_q - 1, k * tq, lhs, mrun)
    m = jnp.max(mrun, axis=1, keepdims=True)
    for qi in range(n_q - 1, -1, -1):
        m_next = None
        if qi > 0:
            lhs = query_operand(qi - 1)
            mrun = neg
        for k in range(qi + 1):
            if k < qi:
                mrun = logits_chunk(qi - 1, k * tq, lhs, mrun)
            probs_chunk(qi, k * tq, m)
        if qi > 0:
            m_next = jnp.max(mrun, axis=1, keepdims=True)
        finish(qi)
        m = m_next

    built = 0
    for i0 in range(0, seq // SUBLANES, CONV_TILES):
        built = _conv_chunk(rot_ref, cw_ref, cb_ref, cv_ref, i0, built)


def _attn_call(q, k, v, qa, ka, u4, cwb, cbb, *, layer):
    b, s, w = q.shape
    n_pairs = w // LANES
    tiles = s // SUBLANES
    head_blk = lambda bi, p: (bi, 0, p)
    shared_blk = lambda bi, p: (bi, 0, 0)
    return pl.pallas_call(
        _attn_kernel,
        grid=(b, n_pairs),
        in_specs=[
            pl.BlockSpec((1, s, LANES), head_blk),
            pl.BlockSpec((1, s, LANES), head_blk),
            pl.BlockSpec((1, s, LANES), head_blk),
            pl.BlockSpec((1, s, LANES), shared_blk),
            pl.BlockSpec((1, s, LANES), shared_blk),
            pl.BlockSpec((1, tiles, SUBLANES, LANES), lambda bi, p: (bi, 0, 0, p)),
            pl.BlockSpec((None, CONV_K, SUBLANES, LANES), lambda bi, p: (layer, 0, 0, p)),
            pl.BlockSpec((None, SUBLANES, LANES), lambda bi, p: (layer, 0, p)),
        ],
        out_specs=[
            pl.BlockSpec((1, s, LANES), head_blk),
            pl.BlockSpec((1, tiles, SUBLANES, LANES), lambda bi, p: (bi, 0, 0, p)),
        ],
        out_shape=[
            jax.ShapeDtypeStruct((b, s, w), jnp.bfloat16),
            jax.ShapeDtypeStruct(u4.shape, jnp.float32),
        ],
        scratch_shapes=[pltpu.VMEM((s, 2 * LANES), jnp.bfloat16),
                        pltpu.VMEM((s, 2 * LANES), jnp.bfloat16),
                        pltpu.VMEM((2, 2 * TQ, s), jnp.float32),
                        pltpu.VMEM((2, 2 * TQ, s), jnp.bfloat16),
                        pltpu.VMEM((SUBLANES, HALO_TILES + tiles, SUBLANES, LANES), jnp.float32)],
        compiler_params=pltpu.CompilerParams(
            dimension_semantics=("arbitrary", "arbitrary"), vmem_limit_bytes=VMEM_LIMIT),
        name="fox_attention_conv",
    )(q, k, v, qa, ka, u4, cwb, cbb)


def _mixffn_kernel(cv_ref, attn_ref, x_ref, mod_ref, lg_ref, lb_ref, wo_ref,
                   gpost_ref, gfpre_ref, win_ref, wout_ref, gfpost_ref, xo_ref, a_ref, xn_ref, h_ref):
    bf16 = jnp.bfloat16
    g1 = mod_ref[0, 2:3, :]
    sh2 = mod_ref[0, 3:4, :]
    sc2 = mod_ref[0, 4:5, :]
    g2 = mod_ref[0, 5:6, :]
    tm = x_ref.shape[0]
    sub = tm // FFN_SUBTILES
    tiles = [slice(r0, r0 + sub) for r0 in range(0, tm, sub)]

    def mixer(rows):
        cv = cv_ref[rows, :]
        mu = jnp.mean(cv, axis=-1, keepdims=True)
        cen = cv - mu
        var = jnp.mean(cen * cen, axis=-1, keepdims=True)
        yl = cen * lax.rsqrt(var + EPS) * lg_ref[...] + lb_ref[...]
        act = (yl * jax.nn.sigmoid(yl)).astype(bf16)
        y = (jnp.dot(attn_ref[rows, :], wo_ref[0:ATTN_WIDTH, :], preferred_element_type=jnp.float32)
             + jnp.dot(act, wo_ref[ATTN_WIDTH:, :], preferred_element_type=jnp.float32))
        xn = x_ref[rows, :] + g1 * _rms(y, gpost_ref[...])
        xn_ref[rows, :] = xn
        h_ref[rows, :] = (_rms(xn, gfpre_ref[...]) * (1.0 + sc2) + sh2).astype(bf16)

    def ffn_chunk(rows, c0):
        h = h_ref[rows, :]
        g = jnp.dot(h, win_ref[:, c0:c0 + FF_CHUNK], preferred_element_type=jnp.float32)
        u = jnp.dot(h, win_ref[:, D_FF + c0:D_FF + c0 + FF_CHUNK], preferred_element_type=jnp.float32)
        a_ref[rows, c0:c0 + FF_CHUNK] = (g * jax.nn.sigmoid(g) * u).astype(bf16)

    def ffn_out(rows):
        return jnp.dot(a_ref[rows, :], wout_ref[...], preferred_element_type=jnp.float32)

    def residual(rows, y2):
        xo_ref[rows, :] = xn_ref[rows, :] + g2 * _rms(y2, gfpost_ref[...])

    chunks = list(range(0, D_FF, FF_CHUNK))
    mixer(tiles[0])
    pending = None
    for t, rows in enumerate(tiles):
        ffn_chunk(rows, chunks[0])
        if pending is not None:
            residual(*pending)
        if t + 1 < len(tiles):
            mixer(tiles[t + 1])
        for c0 in chunks[1:]:
            ffn_chunk(rows, c0)
        pending = (rows, ffn_out(rows))
    residual(*pending)


def _mixffn_call(cv, attn, x2, mod, lg, lb, w_o, gpost, gfpre, w_in, w_out, gfpost, *, layer, seq):
    t, d = x2.shape
    tm = TM_FFN
    tps = seq // tm
    row = lambda i: (i, 0)
    once = pl.Buffered(1)
    resident = lambda a: pl.BlockSpec((None,) + a.shape[1:], lambda i: (layer, 0, 0), pipeline_mode=once)
    return pl.pallas_call(
        _mixffn_kernel,
        grid=(t // tm,),
        in_specs=[
            pl.BlockSpec((tm, CONV_WIDTH), row),
            pl.BlockSpec((tm, ATTN_WIDTH), row),
            pl.BlockSpec((tm, d), row),
            pl.BlockSpec((1, N_MOD, d), lambda i: (i // tps, 0, 0)),
            resident(lg), resident(lb), resident(w_o),
            resident(gpost), resident(gfpre), resident(w_in), resident(w_out), resident(gfpost),
        ],
        out_specs=pl.BlockSpec((tm, d), row),
        out_shape=jax.ShapeDtypeStruct((t, d), jnp.float32),
        scratch_shapes=[pltpu.VMEM((tm, D_FF), jnp.bfloat16),
                        pltpu.VMEM((tm, d), jnp.float32),
                        pltpu.VMEM((tm, d), jnp.bfloat16)],
        compiler_params=pltpu.CompilerParams(
            dimension_semantics=("arbitrary",), vmem_limit_bytes=VMEM_LIMIT),
        name="outproj_ffn",
    )(cv, attn, x2, mod, lg, lb, w_o, gpost, gfpre, w_in, w_out, gfpost)


def _forget_weights(w_f, b_f):
    depth, d, _ = w_f.shape
    keep = (jnp.arange(AUG_STRIDE) < 6).astype(w_f.dtype)
    wf = (jnp.repeat(w_f[..., None], AUG_STRIDE, axis=-1) * keep).reshape(depth, d, N_HEADS * AUG_STRIDE)
    bf = (jnp.repeat(b_f[..., None], AUG_STRIDE, axis=-1) * keep).reshape(depth, 1, N_HEADS * AUG_STRIDE)
    return wf, bf


def kernel(x, c, w_in, b_f, conv_w, conv_b, conv_ln_g, conv_ln_b, w_o, w_ffn_in, w_ffn_out,
           mix_pre_g, mix_post_g, ffn_pre_g, ffn_post_g, ada_w, ada_b):
    b, s, d = x.shape
    depth = w_in.shape[0]
    bf16 = jnp.bfloat16
    aw, cw = ATTN_WIDTH, CONV_WIDTH

    mod_all = _ada_call(c, ada_w, ada_b).reshape(depth, b, N_MOD, d)
    tri = jnp.tri(TM_PROJ // PROJ_SUBTILES, dtype=bf16)
    x2 = x.reshape(b * s, d)

    wqkv = w_in[:, :, 0:3 * aw].astype(bf16)
    wf, bfw = _forget_weights(w_in[:, :, 3 * aw:3 * aw + N_HEADS], b_f)
    wf = wf.astype(bf16)
    wcv = w_in[:, :, 3 * aw + N_HEADS:].astype(bf16)
    w_o_b = w_o.astype(bf16)
    w_fi_b = w_ffn_in.astype(bf16)
    w_fo_b = w_ffn_out.astype(bf16)
    cwb = jnp.broadcast_to(conv_w[:, :, None, :], (depth, CONV_K, SUBLANES, cw))
    cbb = jnp.broadcast_to(conv_b[:, None, :], (depth, SUBLANES, cw))
    vec = lambda a: a.reshape(depth, 1, a.shape[-1])

    for l in range(depth):
        mod = mod_all[l]
        q, k, v, qa, ka, u = _proj_call(
            x2, mod, vec(mix_pre_g), wqkv, wf, bfw, wcv, tri, layer=l, seq=s)
        attn, cv = _attn_call(
            q.reshape(b, s, aw), k.reshape(b, s, aw), v.reshape(b, s, aw),
            qa.reshape(b, s, LANES), ka.reshape(b, s, LANES),
            u.reshape(b, s // SUBLANES, SUBLANES, cw), cwb, cbb, layer=l)
        x2 = _mixffn_call(
            cv.reshape(b * s, cw), attn.reshape(b * s, aw), x2, mod,
            vec(conv_ln_g), vec(conv_ln_b), w_o_b, vec(mix_post_g), vec(ffn_pre_g),
            w_fi_b, w_fo_b, vec(ffn_post_g), layer=l, seq=s)
    return x2.reshape(b, s, d)
```

```python
import functools
import math

import jax
import jax.numpy as jnp
from jax import lax
from jax.experimental import pallas as pl
from jax.experimental.pallas import tpu as pltpu

D_MODEL = 1024
ATTN_WIDTH = 512
HEAD_DIM = 64
N_HEADS = 8
CONV_WIDTH = 512
CONV_K = 31
D_FF = 2816
N_MOD = 6
EPS = 1e-6

LANES = 128
SUBLANES = 8
AUG_STRIDE = 16
NEG_BIG = -1e30
LOG2E = math.log2(math.e)

TM_PROJ = 1024
PROJ_SUBTILES = 4
TQ = 256
TM_FFN = 1024
FF_CHUNK = 256
FFN_SUBTILES = 4
HALO_TILES = 4
CONV_TILES = 8
VMEM_LIMIT = 56 * 1024 * 1024


def _split3(x):
    hi = x.astype(jnp.bfloat16).astype(jnp.float32)
    r = x - hi
    mid = r.astype(jnp.bfloat16).astype(jnp.float32)
    lo = (r - mid).astype(jnp.bfloat16).astype(jnp.float32)
    return hi, mid, lo


def _rms(x, g):
    return x * lax.rsqrt(jnp.mean(x * x, axis=-1, keepdims=True) + EPS) * g


def _ada_kernel(c_ref, w_ref, b_ref, o_ref):
    c = c_ref[...]
    ca = (c * jax.nn.sigmoid(c)).astype(jnp.bfloat16)
    w = w_ref[0].astype(jnp.bfloat16)
    o_ref[0] = jnp.dot(ca, w, preferred_element_type=jnp.float32) + b_ref[0]


def _ada_call(c, ada_w, ada_b):
    depth, d, n = ada_w.shape
    b = c.shape[0]
    tn = 1536
    return pl.pallas_call(
        _ada_kernel,
        grid=(depth, n // tn),
        in_specs=[
            pl.BlockSpec((b, d), lambda l, j: (0, 0)),
            pl.BlockSpec((1, d, tn), lambda l, j: (l, 0, j)),
            pl.BlockSpec((1, 1, tn), lambda l, j: (l, 0, j)),
        ],
        out_specs=pl.BlockSpec((1, b, tn), lambda l, j: (l, 0, j)),
        out_shape=jax.ShapeDtypeStruct((depth, b, n), jnp.float32),
        compiler_params=pltpu.CompilerParams(
            dimension_semantics=("arbitrary", "arbitrary"), vmem_limit_bytes=VMEM_LIMIT),
        name="adaln_mod",
    )(c, ada_w, ada_b.reshape(depth, 1, n))


def _proj_kernel(x_ref, mod_ref, g_ref, wqkv_ref, wf_ref, bf_ref, wcv_ref, tri_ref,
                 q_ref, k_ref, v_ref, qa_ref, ka_ref, u_ref, carry_ref, hb_ref, *, tiles_per_seq):
    si = pl.program_id(0) % tiles_per_seq

    @pl.when(si == 0)
    def _():
        carry_ref[...] = jnp.zeros_like(carry_ref)

    sh = mod_ref[0, 0:1, :]
    sc = mod_ref[0, 1:2, :]
    aw, cw = ATTN_WIDTH, CONV_WIDTH
    sub = tri_ref.shape[0]
    tiles = [slice(r0, r0 + sub) for r0 in range(0, x_ref.shape[0], sub)]

    def pre_norm(rows):
        h = _rms(x_ref[rows, :], g_ref[...]) * (1.0 + sc) + sh
        hb_ref[rows, :] = h.astype(jnp.bfloat16)

    def qkv(rows):
        hb = hb_ref[rows, :]
        q = jnp.dot(hb, wqkv_ref[:, 0:aw], preferred_element_type=jnp.float32)
        q_ref[rows, :] = (q * (HEAD_DIM ** -0.5 * LOG2E)).astype(jnp.bfloat16)
        k_ref[rows, :] = jnp.dot(
            hb, wqkv_ref[:, aw:2 * aw], preferred_element_type=jnp.float32).astype(jnp.bfloat16)
        v_ref[rows, :] = jnp.dot(
            hb, wqkv_ref[:, 2 * aw:3 * aw], preferred_element_type=jnp.float32).astype(jnp.bfloat16)

    def decay_and_glu(rows):
        hb = hb_ref[rows, :]
        fl = jnp.dot(hb, wf_ref[...], preferred_element_type=jnp.float32) + bf_ref[...]
        val = jnp.dot(hb, wcv_ref[:, 0:cw], preferred_element_type=jnp.float32)
        gate = jnp.dot(hb, wcv_ref[:, cw:2 * cw], preferred_element_type=jnp.float32)
        u_ref[rows, :] = val * jax.nn.sigmoid(gate)
        logf = -(jnp.maximum(-fl, 0.0) + jnp.log1p(jnp.exp(-jnp.abs(fl))))
        hi, mid, lo = _split3(logf)
        parts = jnp.concatenate([hi, mid, lo], axis=1).astype(jnp.bfloat16)
        cp = jnp.dot(tri_ref[...], parts, preferred_element_type=jnp.float32)
        cum = cp[:, 0:LANES] + cp[:, LANES:2 * LANES] + cp[:, 2 * LANES:3 * LANES] + carry_ref[...]
        carry_ref[...] = cum[sub - 1:sub, :]

        chi, cmid, clo = _split3(cum * LOG2E)
        one = jnp.ones_like(chi)
        zero = jnp.zeros_like(chi)
        lane = lax.broadcasted_iota(jnp.int32, cum.shape, 1) % AUG_STRIDE
        qa_ref[rows, :] = jnp.where(lane == 0, chi, jnp.where(lane == 1, cmid, jnp.where(
            lane == 2, clo, jnp.where(lane < 6, one, zero)))).astype(qa_ref.dtype)
        ka_ref[rows, :] = jnp.where(lane < 3, one, jnp.where(lane == 3, -chi, jnp.where(
            lane == 4, -cmid, jnp.where(lane == 5, -clo, zero)))).astype(ka_ref.dtype)

    pre_norm(tiles[0])
    for t, rows in enumerate(tiles):
        qkv(rows)
        if t + 1 < len(tiles):
            pre_norm(tiles[t + 1])
        decay_and_glu(rows)


def _proj_call(x2, mod, g_pre, wqkv, wf, bfw, wcv, tri, *, layer, seq):
    t, d = x2.shape
    tm = TM_PROJ
    tps = seq // tm
    row = lambda i: (i, 0)
    bf16 = jnp.bfloat16
    of_layer = lambda a: pl.BlockSpec((None,) + a.shape[1:], lambda i: (layer, 0, 0))
    return pl.pallas_call(
        functools.partial(_proj_kernel, tiles_per_seq=tps),
        grid=(t // tm,),
        in_specs=[
            pl.BlockSpec((tm, d), row),
            pl.BlockSpec((1, N_MOD, d), lambda i: (i // tps, 0, 0)),
            of_layer(g_pre),
            pl.BlockSpec((None, d, 3 * ATTN_WIDTH), lambda i: (layer, 0, 0)),
            of_layer(wf), of_layer(bfw), of_layer(wcv),
            pl.BlockSpec(tri.shape, lambda i: (0, 0)),
        ],
        out_specs=[
            pl.BlockSpec((tm, ATTN_WIDTH), row),
            pl.BlockSpec((tm, ATTN_WIDTH), row),
            pl.BlockSpec((tm, ATTN_WIDTH), row),
            pl.BlockSpec((tm, LANES), row),
            pl.BlockSpec((tm, LANES), row),
            pl.BlockSpec((tm, CONV_WIDTH), row),
        ],
        out_shape=[
            jax.ShapeDtypeStruct((t, ATTN_WIDTH), bf16),
            jax.ShapeDtypeStruct((t, ATTN_WIDTH), bf16),
            jax.ShapeDtypeStruct((t, ATTN_WIDTH), bf16),
            jax.ShapeDtypeStruct((t, LANES), bf16),
            jax.ShapeDtypeStruct((t, LANES), bf16),
            jax.ShapeDtypeStruct((t, CONV_WIDTH), jnp.float32),
        ],
        scratch_shapes=[pltpu.VMEM((1, LANES), jnp.float32),
                        pltpu.VMEM((tm, d), jnp.bfloat16)],
        compiler_params=pltpu.CompilerParams(
            dimension_semantics=("arbitrary",), vmem_limit_bytes=VMEM_LIMIT),
        name="in_proj",
    )(x2, mod, g_pre, wqkv, wf, bfw, wcv, tri)


def _conv_chunk(rot_ref, cw_ref, cb_ref, cv_ref, i0, built):
    n = CONV_TILES
    off0 = SUBLANES * HALO_TILES - (CONV_K - 1)
    need = i0 + n + max((j + off0) // SUBLANES for j in range(CONV_K) if (j + off0) % SUBLANES)
    if need > built:
        nb = need - built
        sub = lax.broadcasted_iota(jnp.int32, (nb, SUBLANES, LANES), 1)
        w = rot_ref[0, built:built + nb + 1]
        for rho in range(1, SUBLANES):
            wr = pltpu.roll(w, SUBLANES - rho, axis=1)
            rot_ref[rho, built:built + nb] = jnp.where(sub < SUBLANES - rho, wr[0:nb], wr[1:nb + 1])
        built = need
    acc = jnp.broadcast_to(cb_ref[...], (n, SUBLANES, LANES))
    for j in range(CONV_K):
        rho, m = (j + off0) % SUBLANES, (j + off0) // SUBLANES
        acc = acc + cw_ref[j] * rot_ref[rho, i0 + m:i0 + m + n]
    cv_ref[0, i0:i0 + n] = acc
    return built


def _attn_kernel(q_ref, k_ref, v_ref, qa_ref, ka_ref, u_ref, cw_ref, cb_ref, o_ref, cv_ref,
                 kk_ref, vv_ref, s_ref, p_ref, rot_ref):
    pair = pl.program_id(1)
    seq = q_ref.shape[1]
    tq = TQ
    n_q = seq // tq
    bf16 = jnp.bfloat16
    nt = (((1,), (1,)), ((), ()))

    kk_ref[:, 0:LANES] = k_ref[0]
    kk_ref[:, LANES:2 * LANES] = ka_ref[0]
    vv_ref[:, 0:LANES] = v_ref[0]
    vv_ref[:, LANES:2 * LANES] = jnp.ones((seq, LANES), bf16)
    rot_ref[0, 0:HALO_TILES] = jnp.zeros((HALO_TILES, SUBLANES, LANES), jnp.float32)
    rot_ref[0, HALO_TILES:] = u_ref[0]

    lane = lax.broadcasted_iota(jnp.int32, (tq, LANES), 1)
    head_of_lane = lane // AUG_STRIDE
    row = lax.broadcasted_iota(jnp.int32, (2 * tq, tq), 0) % tq
    col = lax.broadcasted_iota(jnp.int32, (2 * tq, tq), 1)
    causal = col <= row

    def query_operand(qi):
        q = q_ref[0, qi * tq:(qi + 1) * tq, :].astype(jnp.float32)
        qa = qa_ref[0, qi * tq:(qi + 1) * tq, :].astype(jnp.float32)
        zq = jnp.zeros_like(q)
        return jnp.concatenate([
            jnp.concatenate([jnp.where(lane < HEAD_DIM, q, zq),
                             jnp.where(head_of_lane == 2 * pair, qa, zq)], axis=1),
            jnp.concatenate([jnp.where(lane >= HEAD_DIM, q, zq),
                             jnp.where(head_of_lane == 2 * pair + 1, qa, zq)], axis=1),
        ], axis=0).astype(bf16)

    def logits_chunk(qi, c0, lhs, mrun):
        s = lax.dot_general(lhs, kk_ref[c0:c0 + tq, :], nt, preferred_element_type=jnp.float32)
        if c0 == qi * tq:
            s = jnp.where(causal, s, NEG_BIG)
        s_ref[qi % 2, :, c0:c0 + tq] = s
        for l0 in range(0, tq, LANES):
            mrun = jnp.maximum(mrun, s[:, l0:l0 + LANES])
        return mrun

    def probs_chunk(qi, c0, m):
        p_ref[qi % 2, :, c0:c0 + tq] = jnp.exp2(s_ref[qi % 2, :, c0:c0 + tq] - m).astype(bf16)

    def finish(qi):
        nk = (qi + 1) * tq
        acc = jnp.dot(p_ref[qi % 2, :, 0:nk], vv_ref[0:nk, :], preferred_element_type=jnp.float32)
        denom = acc[:, LANES:LANES + 1]
        o0 = acc[0:tq, 0:LANES] / denom[0:tq]
        o1 = acc[tq:2 * tq, 0:LANES] / denom[tq:2 * tq]
        o_ref[0, qi * tq:(qi + 1) * tq, :] = jnp.where(lane < HEAD_DIM, o0, o1).astype(o_ref.dtype)

    neg = jnp.full((2 * tq, LANES), NEG_BIG, jnp.float32)
    lhs = query_operand(n_q - 1)
    mrun = neg
    for k in range(n_q):
        mrun = logits_chunk(n_q - 1, k * tq, lhs, mrun)
    m = jnp.max(mrun, axis=1, keepdims=True)
    for qi in range(n_q - 1, -1, -1):
        m_next = None
        if qi > 0:
            lhs = query_operand(qi - 1)
            mrun = neg
        for k in range(qi + 1):
            if k < qi:
                mrun = logits_chunk(qi - 1, k * tq, lhs, mrun)
            probs_chunk(qi, k * tq, m)
        if qi > 0:
            m_next = jnp.max(mrun, axis=1, keepdims=True)
        finish(qi)
        m = m_next

    built = 0
    for i0 in range(0, seq // SUBLANES, CONV_TILES):
        built = _conv_chunk(rot_ref, cw_ref, cb_ref, cv_ref, i0, built)


def _attn_call(q, k, v, qa, ka, u4, cwb, cbb, *, layer):
    b, s, w = q.shape
    n_pairs = w // LANES
    tiles = s // SUBLANES
    head_blk = lambda bi, p: (bi, 0, p)
    shared_blk = lambda bi, p: (bi, 0, 0)
    return pl.pallas_call(
        _attn_kernel,
        grid=(b, n_pairs),
        in_specs=[
            pl.BlockSpec((1, s, LANES), head_blk),
            pl.BlockSpec((1, s, LANES), head_blk),
            pl.BlockSpec((1, s, LANES), head_blk),
            pl.BlockSpec((1, s, LANES), shared_blk),
            pl.BlockSpec((1, s, LANES), shared_blk),
            pl.BlockSpec((1, tiles, SUBLANES, LANES), lambda bi, p: (bi, 0, 0, p)),
            pl.BlockSpec((None, CONV_K, SUBLANES, LANES), lambda bi, p: (layer, 0, 0, p)),
            pl.BlockSpec((None, SUBLANES, LANES), lambda bi, p: (layer, 0, p)),
        ],
        out_specs=[
            pl.BlockSpec((1, s, LANES), head_blk),
            pl.BlockSpec((1, tiles, SUBLANES, LANES), lambda bi, p: (bi, 0, 0, p)),
        ],
        out_shape=[
            jax.ShapeDtypeStruct((b, s, w), jnp.bfloat16),
            jax.ShapeDtypeStruct(u4.shape, jnp.float32),
        ],
        scratch_shapes=[pltpu.VMEM((s, 2 * LANES), jnp.bfloat16),
                        pltpu.VMEM((s, 2 * LANES), jnp.bfloat16),
                        pltpu.VMEM((2, 2 * TQ, s), jnp.float32),
                        pltpu.VMEM((2, 2 * TQ, s), jnp.bfloat16),
                        pltpu.VMEM((SUBLANES, HALO_TILES + tiles, SUBLANES, LANES), jnp.float32)],
        compiler_params=pltpu.CompilerParams(
            dimension_semantics=("arbitrary", "arbitrary"), vmem_limit_bytes=VMEM_LIMIT),
        name="fox_attention_conv",
    )(q, k, v, qa, ka, u4, cwb, cbb)


def _mixffn_kernel(cv_ref, attn_ref, x_ref, mod_ref, lg_ref, lb_ref, wo_ref,
                   gpost_ref, gfpre_ref, win_ref, wout_ref, gfpost_ref, xo_ref, a_ref, xn_ref, h_ref):
    bf16 = jnp.bfloat16
    g1 = mod_ref[0, 2:3, :]
    sh2 = mod_ref[0, 3:4, :]
    sc2 = mod_ref[0, 4:5, :]
    g2 = mod_ref[0, 5:6, :]
    tm = x_ref.shape[0]
    sub = tm // FFN_SUBTILES
    tiles = [slice(r0, r0 + sub) for r0 in range(0, tm, sub)]

    def mixer(rows):
        cv = cv_ref[rows, :]
        mu = jnp.mean(cv, axis=-1, keepdims=True)
        cen = cv - mu
        var = jnp.mean(cen * cen, axis=-1, keepdims=True)
        yl = cen * lax.rsqrt(var + EPS) * lg_ref[...] + lb_ref[...]
        act = (yl * jax.nn.sigmoid(yl)).astype(bf16)
        y = (jnp.dot(attn_ref[rows, :], wo_ref[0:ATTN_WIDTH, :], preferred_element_type=jnp.float32)
             + jnp.dot(act, wo_ref[ATTN_WIDTH:, :], preferred_element_type=jnp.float32))
        xn = x_ref[rows, :] + g1 * _rms(y, gpost_ref[...])
        xn_ref[rows, :] = xn
        h_ref[rows, :] = (_rms(xn, gfpre_ref[...]) * (1.0 + sc2) + sh2).astype(bf16)

    def ffn_chunk(rows, c0):
        h = h_ref[rows, :]
        g = jnp.dot(h, win_ref[:, c0:c0 + FF_CHUNK], preferred_element_type=jnp.float32)
        u = jnp.dot(h, win_ref[:, D_FF + c0:D_FF + c0 + FF_CHUNK], preferred_element_type=jnp.float32)
        a_ref[rows, c0:c0 + FF_CHUNK] = (g * jax.nn.sigmoid(g) * u).astype(bf16)

    def ffn_out(rows):
        return jnp.dot(a_ref[rows, :], wout_ref[...], preferred_element_type=jnp.float32)

    def residual(rows, y2):
        xo_ref[rows, :] = xn_ref[rows, :] + g2 * _rms(y2, gfpost_ref[...])

    chunks = list(range(0, D_FF, FF_CHUNK))
    mixer(tiles[0])
    pending = None
    for t, rows in enumerate(tiles):
        ffn_chunk(rows, chunks[0])
        if pending is not None:
            residual(*pending)
        if t + 1 < len(tiles):
            mixer(tiles[t + 1])
        for c0 in chunks[1:]:
            ffn_chunk(rows, c0)
        pending = (rows, ffn_out(rows))
    residual(*pending)


def _mixffn_call(cv, attn, x2, mod, lg, lb, w_o, gpost, gfpre, w_in, w_out, gfpost, *, layer, seq):
    t, d = x2.shape
    tm = TM_FFN
    tps = seq // tm
    row = lambda i: (i, 0)
    once = pl.Buffered(1)
    resident = lambda a: pl.BlockSpec((None,) + a.shape[1:], lambda i: (layer, 0, 0), pipeline_mode=once)
    return pl.pallas_call(
        _mixffn_kernel,
        grid=(t // tm,),
        in_specs=[
            pl.BlockSpec((tm, CONV_WIDTH), row),
            pl.BlockSpec((tm, ATTN_WIDTH), row),
            pl.BlockSpec((tm, d), row),
            pl.BlockSpec((1, N_MOD, d), lambda i: (i // tps, 0, 0)),
            resident(lg), resident(lb), resident(w_o),
            resident(gpost), resident(gfpre), resident(w_in), resident(w_out), resident(gfpost),
        ],
        out_specs=pl.BlockSpec((tm, d), row),
        out_shape=jax.ShapeDtypeStruct((t, d), jnp.float32),
        scratch_shapes=[pltpu.VMEM((tm, D_FF), jnp.bfloat16),
                        pltpu.VMEM((tm, d), jnp.float32),
                        pltpu.VMEM((tm, d), jnp.bfloat16)],
        compiler_params=pltpu.CompilerParams(
            dimension_semantics=("arbitrary",), vmem_limit_bytes=VMEM_LIMIT),
        name="outproj_ffn",
    )(cv, attn, x2, mod, lg, lb, w_o, gpost, gfpre, w_in, w_out, gfpost)


def _forget_weights(w_f, b_f):
    depth, d, _ = w_f.shape
    keep = (jnp.arange(AUG_STRIDE) < 6).astype(w_f.dtype)
    wf = (jnp.repeat(w_f[..., None], AUG_STRIDE, axis=-1) * keep).reshape(depth, d, N_HEADS * AUG_STRIDE)
    bf = (jnp.repeat(b_f[..., None], AUG_STRIDE, axis=-1) * keep).reshape(depth, 1, N_HEADS * AUG_STRIDE)
    return wf, bf


def kernel(x, c, w_in, b_f, conv_w, conv_b, conv_ln_g, conv_ln_b, w_o, w_ffn_in, w_ffn_out,
           mix_pre_g, mix_post_g, ffn_pre_g, ffn_post_g, ada_w, ada_b):
    b, s, d = x.shape
    depth = w_in.shape[0]
    bf16 = jnp.bfloat16
    aw, cw = ATTN_WIDTH, CONV_WIDTH

    mod_all = _ada_call(c, ada_w, ada_b).reshape(depth, b, N_MOD, d)
    tri = jnp.tri(TM_PROJ // PROJ_SUBTILES, dtype=bf16)
    x2 = x.reshape(b * s, d)

    w_in_b = w_in.astype(bf16)
    wf, bfw = _forget_weights(w_in[:, :, 3 * aw:3 * aw + N_HEADS], b_f)
    wf = wf.astype(bf16)
    wcv = w_in_b[:, :, 3 * aw + N_HEADS:]
    w_o_b = w_o.astype(bf16)
    w_fi_b = w_ffn_in.astype(bf16)
    w_fo_b = w_ffn_out.astype(bf16)
    cwb = jnp.broadcast_to(conv_w[:, :, None, :], (depth, CONV_K, SUBLANES, cw))
    cbb = jnp.broadcast_to(conv_b[:, None, :], (depth, SUBLANES, cw))
    vec = lambda a: a.reshape(depth, 1, a.shape[-1])

    for l in range(depth):
        mod = mod_all[l]
        q, k, v, qa, ka, u = _proj_call(
            x2, mod, vec(mix_pre_g), w_in_b, wf, bfw, wcv, tri, layer=l, seq=s)
        attn, cv = _attn_call(
            q.reshape(b, s, aw), k.reshape(b, s, aw), v.reshape(b, s, aw),
            qa.reshape(b, s, LANES), ka.reshape(b, s, LANES),
            u.reshape(b, s // SUBLANES, SUBLANES, cw), cwb, cbb, layer=l)
        x2 = _mixffn_call(
            cv.reshape(b * s, cw), attn.reshape(b * s, aw), x2, mod,
            vec(conv_ln_g), vec(conv_ln_b), w_o_b, vec(mix_post_g), vec(ffn_pre_g),
            w_fi_b, w_fo_b, vec(ffn_post_g), layer=l, seq=s)
    return x2.reshape(b, s, d)
```
